```python
import math
import jax, jax.numpy as jnp
from jax import lax
import numpy as np

D_MODEL = 2048
BATCH = 4
SEQ = 4096
DEPTH = 2

D_PLE = 256
D_FF = 5632
MLA_HEADS = 8
MLA_NOPE = 128
MLA_ROPE = 64
MLA_QK = MLA_NOPE + MLA_ROPE
MLA_V = 128
Q_LORA = 512
KV_LORA = 256
ROPE_THETA = 10000.0
CONV_CH = 512
CONV_WIDTH = 31
SB_HEADS = 4
SB_HEAD_DIM = 128
D_MIX = MLA_HEADS * MLA_V + CONV_CH + SB_HEADS * SB_HEAD_DIM
D_SB_QKV = 3 * SB_HEADS * SB_HEAD_DIM
D_IN = Q_LORA + KV_LORA + MLA_ROPE + 2 * CONV_CH + D_SB_QKV
IN_SPLITS = (Q_LORA, Q_LORA + KV_LORA, Q_LORA + KV_LORA + MLA_ROPE,
             Q_LORA + KV_LORA + MLA_ROPE + 2 * CONV_CH)
BLOCK_Q = 128
EPS = 1e-6
NEG = -1e30

kernel_name = "hymba_mla_conformer_stickbreak_macaron"


def rmsnorm(x, g):
    xf = x.astype(jnp.float32)
    y = xf * lax.rsqrt(jnp.mean(xf * xf, axis=-1, keepdims=True) + EPS)
    return (y * g.astype(jnp.float32)).astype(x.dtype)


def layernorm(x, g, b):
    xf = x.astype(jnp.float32)
    mu = jnp.mean(xf, axis=-1, keepdims=True)
    var = jnp.mean(jnp.square(xf - mu), axis=-1, keepdims=True)
    y = (xf - mu) * lax.rsqrt(var + EPS)
    return (y * g.astype(jnp.float32) + b.astype(jnp.float32)).astype(x.dtype)


def swiglu(h, w_in, w_out):
    a, u = jnp.split(h @ w_in, 2, axis=-1)
    return (jax.nn.silu(a) * u) @ w_out


def rope(x, positions):
    d = x.shape[-1]
    inv_freq = ROPE_THETA ** (-jnp.arange(0, d, 2, dtype=jnp.float32) / d)
    ang = positions.astype(jnp.float32)[..., None] * inv_freq
    cos = jnp.cos(ang)[:, :, None, :].astype(x.dtype)
    sin = jnp.sin(ang)[:, :, None, :].astype(x.dtype)
    x1, x2 = jnp.split(x, 2, axis=-1)
    return jnp.concatenate([x1 * cos - x2 * sin, x1 * sin + x2 * cos], axis=-1)


def query_blocks(q):
    b, s, h, d = q.shape
    return q.reshape(b, s // BLOCK_Q, BLOCK_Q, h, d).transpose(1, 0, 2, 3, 4)


def merge_blocks(o):
    nb, b, bq, h, d = o.shape
    return o.transpose(1, 0, 2, 3, 4).reshape(b, nb * bq, h * d)


def causal_softmax_attention(q, k, v):
    s_len = q.shape[1]
    scale = q.shape[-1] ** -0.5
    kpos = jnp.arange(s_len)

    def one_block(args):
        qb, blk = args
        qpos = blk * BLOCK_Q + jnp.arange(BLOCK_Q)
        s = jnp.einsum('bqhd,bkhd->bhqk', qb, k, preferred_element_type=jnp.float32) * scale
        s = jnp.where(kpos[None, :] <= qpos[:, None], s, NEG)
        w = jax.nn.softmax(s, axis=-1)
        return jnp.einsum('bhqk,bkhd->bqhd', w.astype(v.dtype), v)

    out = lax.map(one_block, (query_blocks(q), jnp.arange(s_len // BLOCK_Q)))
    return merge_blocks(out)


def stick_breaking_attention(q, k, v):
    s_len = q.shape[1]
    scale = q.shape[-1] ** -0.5
    kpos = jnp.arange(s_len)

    def one_block(args):
        qb, blk = args
        qpos = blk * BLOCK_Q + jnp.arange(BLOCK_Q)
        z = jnp.einsum('bqhd,bkhd->bhqk', qb, k, preferred_element_type=jnp.float32) * scale
        strict = kpos[None, :] < qpos[:, None]
        log_beta = jax.nn.log_sigmoid(z)
        log_keep = jnp.where(strict, jax.nn.log_sigmoid(-z), 0.0)
        after = lax.cumsum(log_keep, axis=log_keep.ndim - 1, reverse=True) - log_keep
        w = jnp.where(strict, jnp.exp(log_beta + after), 0.0)
        return jnp.einsum('bhqk,bkhd->bqhd', w.astype(v.dtype), v)

    out = lax.map(one_block, (query_blocks(q), jnp.arange(s_len // BLOCK_Q)))
    return merge_blocks(out)


def conformer_conv(u, w_dw, b_dw, g_ln, b_ln, w_pw):
    a, gate = jnp.split(u, 2, axis=-1)
    g = a * jax.nn.sigmoid(gate)
    y = lax.conv_general_dilated(
        g, w_dw[:, None, :].astype(g.dtype), window_strides=(1,),
        padding=[(CONV_WIDTH - 1, 0)], dimension_numbers=('NWC', 'WIO', 'NWC'),
        feature_group_count=CONV_CH) + b_dw.astype(g.dtype)
    y = layernorm(y, g_ln, b_ln)
    return jax.nn.silu(y) @ w_pw


def setup_inputs(seed: int = 0) -> dict:
    key = jax.random.key(seed)
    ks = iter(jax.random.split(key, 40))

    def w(shape, fan_in):
        return jax.random.normal(next(ks), shape, jnp.float32) * (fan_in ** -0.5)

    def gain(n):
        return 1.0 + 0.02 * jax.random.normal(next(ks), (DEPTH, n), jnp.float32)

    def bias(n):
        return 0.01 * jax.random.normal(next(ks), (DEPTH, n), jnp.float32)

    x = jax.random.normal(next(ks), (BATCH, SEQ, D_MODEL), jnp.float32)
    p = jax.random.normal(next(ks), (DEPTH, BATCH, SEQ, D_PLE), jnp.float32)
    positions = jnp.broadcast_to(jnp.arange(SEQ, dtype=jnp.int32), (BATCH, SEQ))
    return {
        "x": x, "p": p, "positions": positions,
        "g_ff1_pre": gain(D_MODEL),
        "w_ff1_in": w((DEPTH, D_MODEL, 2 * D_FF), D_MODEL),
        "w_ff1_out": w((DEPTH, D_FF, D_MODEL), D_FF),
        "g_ff1_post": gain(D_MODEL),
        "g_mix_pre": gain(D_MODEL),
        "w_in": w((DEPTH, D_MODEL, D_IN), D_MODEL),
        "g_cq": gain(Q_LORA),
        "w_uq": w((DEPTH, Q_LORA, MLA_HEADS * MLA_QK), Q_LORA),
        "g_ckv": gain(KV_LORA),
        "w_ukv": w((DEPTH, KV_LORA, MLA_HEADS * (MLA_NOPE + MLA_V)), KV_LORA),
        "w_dw": w((DEPTH, CONV_WIDTH, CONV_CH), CONV_WIDTH),
        "b_dw": bias(CONV_CH),
        "g_conv_ln": gain(CONV_CH),
        "b_conv_ln": bias(CONV_CH),
        "w_pw": w((DEPTH, CONV_CH, CONV_CH), CONV_CH),
        "w_out": w((DEPTH, D_MIX, D_MODEL), D_MIX),
        "g_mix_post": gain(D_MODEL),
        "g_ff2_pre": gain(D_MODEL),
        "w_ff2_in": w((DEPTH, D_MODEL, 2 * D_FF), D_MODEL),
        "w_ff2_out": w((DEPTH, D_FF, D_MODEL), D_FF),
        "g_ff2_post": gain(D_MODEL),
        "g_ple_pre": gain(D_MODEL),
        "w_ple_gate": w((DEPTH, D_MODEL, D_MODEL), D_MODEL),
        "w_ple_proj": w((DEPTH, D_PLE, D_MODEL), D_PLE),
        "g_ple_post": gain(D_MODEL),
    }


def reference(x, p, positions,
              g_ff1_pre, w_ff1_in, w_ff1_out, g_ff1_post,
              g_mix_pre, w_in, g_cq, w_uq, g_ckv, w_ukv,
              w_dw, b_dw, g_conv_ln, b_conv_ln, w_pw, w_out, g_mix_post,
              g_ff2_pre, w_ff2_in, w_ff2_out, g_ff2_post,
              g_ple_pre, w_ple_gate, w_ple_proj, g_ple_post):
    b, s, _ = x.shape
    for i in range(DEPTH):
        f = swiglu(rmsnorm(x, g_ff1_pre[i]), w_ff1_in[i], w_ff1_out[i])
        x = x + 0.5 * rmsnorm(f, g_ff1_post[i])

        h = rmsnorm(x, g_mix_pre[i])
        u = h @ w_in[i]
        c_q, c_kv, k_r, u_conv, u_sb = jnp.split(u, IN_SPLITS, axis=-1)

        q = (rmsnorm(c_q, g_cq[i]) @ w_uq[i]).reshape(b, s, MLA_HEADS, MLA_QK)
        q_nope, q_rope = jnp.split(q, [MLA_NOPE], axis=-1)
        q = jnp.concatenate([q_nope, rope(q_rope, positions)], axis=-1)
        kv = (rmsnorm(c_kv, g_ckv[i]) @ w_ukv[i]).reshape(b, s, MLA_HEADS, MLA_NOPE + MLA_V)
        k_nope, v = jnp.split(kv, [MLA_NOPE], axis=-1)
        k_rope = rope(k_r[:, :, None, :], positions)
        k = jnp.concatenate(
            [k_nope, jnp.broadcast_to(k_rope, (b, s, MLA_HEADS, MLA_ROPE))], axis=-1)
        o_mla = causal_softmax_attention(q, k, v)

        o_conv = conformer_conv(u_conv, w_dw[i], b_dw[i], g_conv_ln[i], b_conv_ln[i], w_pw[i])

        qkv = u_sb.reshape(b, s, 3, SB_HEADS, SB_HEAD_DIM)
        o_sb = stick_breaking_attention(qkv[:, :, 0], qkv[:, :, 1], qkv[:, :, 2])

        mix = jnp.concatenate([o_mla, o_conv, o_sb], axis=-1) @ w_out[i]
        x = x + rmsnorm(mix, g_mix_post[i])

        f = swiglu(rmsnorm(x, g_ff2_pre[i]), w_ff2_in[i], w_ff2_out[i])
        x = x + 0.5 * rmsnorm(f, g_ff2_post[i])

        gate = jax.nn.sigmoid(rmsnorm(x, g_ple_pre[i]) @ w_ple_gate[i])
        e = p[i].astype(x.dtype) @ w_ple_proj[i]
        x = x + rmsnorm(gate * e, g_ple_post[i])
    return x
```

```python
import functools

import jax
import jax.numpy as jnp
from jax import lax
from jax.experimental import pallas as pl
from jax.experimental.pallas import tpu as pltpu

F32 = jnp.float32
BF16 = jnp.bfloat16

D_MODEL = 2048
D_PLE = 256
D_FF = 5632
MLA_HEADS = 8
MLA_NOPE = 128
MLA_ROPE = 64
MLA_QK = MLA_NOPE + MLA_ROPE
MLA_V = 128
Q_LORA = 512
KV_LORA = 256
ROPE_THETA = 10000.0
CONV_CH = 512
CONV_WIDTH = 31
SB_HEADS = 4
SB_HEAD_DIM = 128
D_MIX = MLA_HEADS * MLA_V + CONV_CH + SB_HEADS * SB_HEAD_DIM
EPS = 1e-6
NEG = -1e30

LANES = 128
CONV_HALO = 32

_U_CQ = 0
_U_CKV = Q_LORA
_U_KR = Q_LORA + KV_LORA
_U_CONV = _U_KR + 2 * MLA_ROPE
_U_SB = _U_CONV + 2 * CONV_CH
_U_END = _U_SB + 3 * SB_HEADS * SB_HEAD_DIM

VMEM_LIMIT = 48 * 1024 * 1024


def _params(*sem):
    return pltpu.CompilerParams(dimension_semantics=sem, vmem_limit_bytes=VMEM_LIMIT)


def _resident(shape):
    nd = len(shape)
    return pl.BlockSpec(shape, lambda *_: (0,) * nd, pipeline_mode=pl.Buffered(1))


def _rms(x, g):
    return x * lax.rsqrt(jnp.mean(x * x, axis=-1, keepdims=True) + EPS) * g


def _rope_table_kernel(pos_ref, freq_ref, cos_ref, sin_ref):
    ang = pos_ref[...].astype(F32) * freq_ref[...]
    lane = lax.broadcasted_iota(jnp.int32, ang.shape, 1)
    first_half = (lane & (MLA_ROPE - 1)) < (MLA_ROPE // 2)
    cos_ref[...] = jnp.cos(ang)
    sin_ref[...] = jnp.where(first_half, -jnp.sin(ang), jnp.sin(ang))


def _rope_tables(positions, tm=1024):
    t = positions.size
    half = MLA_ROPE // 2
    inv_freq = ROPE_THETA ** (-jnp.arange(0, MLA_ROPE, 2, dtype=F32) / MLA_ROPE)
    freq = jnp.tile(inv_freq, LANES // half)[None, :]
    pos = positions.reshape(t, 1)
    return pl.pallas_call(
        _rope_table_kernel,
        grid=(t // tm,),
        in_specs=[pl.BlockSpec((tm, 1), lambda i: (i, 0)), _resident((1, LANES))],
        out_specs=[pl.BlockSpec((tm, LANES), lambda i: (i, 0))] * 2,
        out_shape=[jax.ShapeDtypeStruct((t, LANES), F32)] * 2,
        compiler_params=_params("parallel"),
        name="rope_tables",
    )(pos, freq)


def _ffn_kernel(x_ref, gpre_ref, wa_ref, wu_ref, wo_ref, gpost_ref, o_ref, h_ref, acc_ref):
    j = pl.program_id(1)

    @pl.when(j == 0)
    def _():
        h_ref[...] = _rms(x_ref[...], gpre_ref[...]).astype(BF16)

    h = h_ref[...]
    a = jnp.dot(h, wa_ref[...], preferred_element_type=F32)
    u = jnp.dot(h, wu_ref[...], preferred_element_type=F32)
    act = (a * jax.nn.sigmoid(a) * u).astype(BF16)
    part = jnp.dot(act, wo_ref[...], preferred_element_type=F32)

    @pl.when(j == 0)
    def _():
        acc_ref[...] = part

    @pl.when(j > 0)
    def _():
        acc_ref[...] += part

    @pl.when(j == pl.num_programs(1) - 1)
    def _():
        o_ref[...] = x_ref[...] + 0.5 * _rms(acc_ref[...], gpost_ref[...])


def _ffn(x, g_pre, w_in, w_out, g_post, tm=512, tf=512):
    t = x.shape[0]
    nf = D_FF // tf
    return pl.pallas_call(
        _ffn_kernel,
        grid=(t // tm, nf),
        in_specs=[
            pl.BlockSpec((tm, D_MODEL), lambda i, j: (i, 0)),
            _resident((1, D_MODEL)),
            pl.BlockSpec((D_MODEL, tf), lambda i, j: (0, j)),
            pl.BlockSpec((D_MODEL, tf), lambda i, j: (0, nf + j)),
            pl.BlockSpec((tf, D_MODEL), lambda i, j: (j, 0)),
            _resident((1, D_MODEL)),
        ],
        out_specs=pl.BlockSpec((tm, D_MODEL), lambda i, j: (i, 0)),
        out_shape=jax.ShapeDtypeStruct((t, D_MODEL), F32),
        scratch_shapes=[pltpu.VMEM((tm, D_MODEL), BF16), pltpu.VMEM((tm, D_MODEL), F32)],
        compiler_params=_params("parallel", "arbitrary"),
        name="ffn",
    )(x, g_pre, w_in, w_in, w_out, g_post)


def _proj_kernel(x_ref, gpre_ref, win_ref, gcq_ref, wuq_ref, gckv_ref, wukv_ref, cos_ref, sin_ref,
                 qn_ref, qr_ref, kn_ref, kr_ref, v_ref, g_ref, sq_ref, sk_ref, sv_ref):
    h = _rms(x_ref[...], gpre_ref[...]).astype(BF16)
    u = jnp.dot(h, win_ref[...], preferred_element_type=F32)
    cos = cos_ref[...]
    sin = sin_ref[...]
    lane = lax.broadcasted_iota(jnp.int32, cos.shape, 1)
    upper = lane >= MLA_ROPE

    cq = _rms(u[:, _U_CQ:_U_CKV], gcq_ref[...]).astype(BF16)
    q = jnp.dot(cq, wuq_ref[...], preferred_element_type=F32)
    n_nope = MLA_HEADS * MLA_NOPE
    n_rope = MLA_HEADS * MLA_ROPE
    qn_ref[...] = q[:, :n_nope].astype(BF16)
    for pair in range(MLA_HEADS // 2):
        lo = n_nope + pair * LANES
        roped = q[:, lo:lo + LANES] * cos + q[:, lo + n_rope:lo + n_rope + LANES] * sin
        qr_ref[:, (2 * pair) * LANES:(2 * pair + 1) * LANES] = jnp.where(upper, 0.0, roped).astype(BF16)
        qr_ref[:, (2 * pair + 1) * LANES:(2 * pair + 2) * LANES] = jnp.where(upper, roped, 0.0).astype(BF16)

    ckv = _rms(u[:, _U_CKV:_U_KR], gckv_ref[...]).astype(BF16)
    kv = jnp.dot(ckv, wukv_ref[...], preferred_element_type=F32)
    kn_ref[...] = kv[:, :n_nope].astype(BF16)
    v_ref[...] = kv[:, n_nope:].astype(BF16)
    t = u[:, _U_KR:_U_CONV] * jnp.where(upper, sin, cos)
    kr_ref[...] = (t + pltpu.roll(t, MLA_ROPE, 1)).astype(BF16)

    a = u[:, _U_CONV:_U_CONV + CONV_CH]
    gate = u[:, _U_CONV + CONV_CH:_U_SB]
    g_ref[...] = a * jax.nn.sigmoid(gate)

    n_sb = SB_HEADS * SB_HEAD_DIM
    sq_ref[...] = u[:, _U_SB:_U_SB + n_sb].astype(BF16)
    sk_ref[...] = u[:, _U_SB + n_sb:_U_SB + 2 * n_sb].astype(BF16)
    sv_ref[...] = u[:, _U_SB + 2 * n_sb:_U_END].astype(BF16)


def _proj(x, g_pre, w_in_ext, g_cq, w_uq_ext, g_ckv, w_ukv_ext, cos_t, sin_t, tm=256):
    t = x.shape[0]
    n_nope = MLA_HEADS * MLA_NOPE
    n_sb = SB_HEADS * SB_HEAD_DIM
    row = lambda n: pl.BlockSpec((tm, n), lambda i: (i, 0))
    out_cols = [(n_nope, BF16), (n_nope, BF16), (n_nope, BF16), (LANES, BF16), (n_nope, BF16),
                (CONV_CH, F32), (n_sb, BF16), (n_sb, BF16), (n_sb, BF16)]
    return pl.pallas_call(
        _proj_kernel,
        grid=(t // tm,),
        in_specs=[
            row(D_MODEL), _resident((1, D_MODEL)), _resident(w_in_ext.shape),
            _resident((1, Q_LORA)), _resident(w_uq_ext.shape),
            _resident((1, KV_LORA)), _resident(w_ukv_ext.shape),
            row(LANES), row(LANES),
        ],
        out_specs=[row(n) for n, _ in out_cols],
        out_shape=[jax.ShapeDtypeStruct((t, n), dt) for n, dt in out_cols],
        compiler_params=_params("parallel"),
        name="mixer_proj",
    )(x, g_pre, w_in_ext, g_cq, w_uq_ext, g_ckv, w_ukv_ext, cos_t, sin_t)


def _mla_kernel(qn_ref, qr_ref, kn_ref, kr_ref, v_ref, o_ref, m_ref, l_ref, acc_ref, *, tile):
    qi = pl.program_id(2)
    q = jnp.concatenate([qn_ref[...], qr_ref[...]], axis=-1)
    scale = MLA_QK ** -0.5
    m_ref[...] = jnp.full(m_ref.shape, -jnp.inf, F32)
    l_ref[...] = jnp.zeros(l_ref.shape, F32)
    acc_ref[...] = jnp.zeros(acc_ref.shape, F32)

    def step(kj, diagonal):
        ks = pl.multiple_of(kj * tile, tile)
        k = jnp.concatenate([kn_ref[pl.ds(ks, tile), :], kr_ref[pl.ds(ks, tile), :]], axis=-1)
        s = lax.dot_general(q, k, (((1,), (1,)), ((), ())), preferred_element_type=F32) * scale
        if diagonal:
            row = lax.broadcasted_iota(jnp.int32, s.shape, 0)
            col = lax.broadcasted_iota(jnp.int32, s.shape, 1)
            s = jnp.where(col <= row, s, NEG)
        m_prev = m_ref[...]
        m_new = jnp.maximum(m_prev, jnp.max(s, axis=-1, keepdims=True))
        alpha = jnp.exp(m_prev - m_new)
        p = jnp.exp(s - m_new)
        l_ref[...] = alpha * l_ref[...] + jnp.sum(p, axis=-1, keepdims=True)
        pv = jnp.dot(p.astype(BF16), v_ref[pl.ds(ks, tile), :], preferred_element_type=F32)
        acc_ref[...] = alpha * acc_ref[...] + pv
        m_ref[...] = m_new

    def body(kj, carry):
        step(kj, False)
        return carry

    lax.fori_loop(0, qi, body, 0)
    step(qi, True)
    o_ref[...] = (acc_ref[...] / l_ref[...]).astype(BF16)


def _mla_attention(qn, qr, kn, kr, v, batch, seq, tile=256):
    t = qn.shape[0]
    nq = seq // tile
    qspec = pl.BlockSpec((tile, LANES), lambda b, h, i: (b * nq + i, h))
    kspec = pl.BlockSpec((seq, LANES), lambda b, h, i: (b, h))
    return pl.pallas_call(
        functools.partial(_mla_kernel, tile=tile),
        grid=(batch, MLA_HEADS, nq),
        in_specs=[qspec, qspec, kspec, pl.BlockSpec((seq, LANES), lambda b, h, i: (b, 0)), kspec],
        out_specs=qspec,
        out_shape=jax.ShapeDtypeStruct((t, MLA_HEADS * MLA_V), BF16),
        scratch_shapes=[pltpu.VMEM((tile, 1), F32), pltpu.VMEM((tile, 1), F32), pltpu.VMEM((tile, MLA_V), F32)],
        compiler_params=_params("parallel", "parallel", "arbitrary"),
        name="mla_attention",
    )(qn, qr, kn, kr, v)


def _sb_kernel(q_ref, k_ref, v_ref, o_ref, run_ref, acc_ref, *, tile):
    qi = pl.program_id(2)
    q = q_ref[...]
    scale = SB_HEAD_DIM ** -0.5
    uj = lax.broadcasted_iota(jnp.int32, (tile, tile + LANES), 0)
    us = lax.broadcasted_iota(jnp.int32, (tile, tile + LANES), 1)
    suffix = jnp.where((uj > us) | (us >= tile), 1.0, 0.0).astype(BF16)
    run_ref[...] = jnp.zeros(run_ref.shape, F32)
    acc_ref[...] = jnp.zeros(acc_ref.shape, F32)

    def step(kj, diagonal):
        ks = pl.multiple_of(kj * tile, tile)
        z = lax.dot_general(q, k_ref[pl.ds(ks, tile), :], (((1,), (1,)), ((), ())),
                            preferred_element_type=F32) * scale
        softplus = jnp.log1p(jnp.exp(-jnp.abs(z)))
        log_beta = jnp.minimum(z, 0.0) - softplus
        log_keep = log_beta - z
        if diagonal:
            row = lax.broadcasted_iota(jnp.int32, z.shape, 0)
            col = lax.broadcasted_iota(jnp.int32, z.shape, 1)
            strict = col < row
            log_keep = jnp.where(strict, log_keep, 0.0)
        hi = log_keep.astype(BF16)
        lo = (log_keep - hi.astype(F32)).astype(BF16)
        sums = (jnp.dot(hi, suffix, preferred_element_type=F32)
                + jnp.dot(lo, suffix, preferred_element_type=F32))
        run = run_ref[...]
        after = sums[:, :tile] + jnp.concatenate([run] * (tile // LANES), axis=-1)
        w = jnp.exp(log_beta + after)
        if diagonal:
            w = jnp.where(strict, w, 0.0)
        acc_ref[...] += jnp.dot(w.astype(BF16), v_ref[pl.ds(ks, tile), :], preferred_element_type=F32)
        run_ref[...] = run + sums[:, tile:]

    step(qi, True)

    def body(i, carry):
        step(qi - 1 - i, False)
        return carry

    lax.fori_loop(0, qi, body, 0)
    o_ref[...] = acc_ref[...].astype(BF16)


def _sb_attention(q, k, v, batch, seq, tile=256):
    t = q.shape[0]
    nq = seq // tile
    qspec = pl.BlockSpec((tile, LANES), lambda b, h, i: (b * nq + i, h))
    kspec = pl.BlockSpec((seq, LANES), lambda b, h, i: (b, h))
    return pl.pallas_call(
        functools.partial(_sb_kernel, tile=tile),
        grid=(batch, SB_HEADS, nq),
        in_specs=[qspec, kspec, kspec],
        out_specs=qspec,
        out_shape=jax.ShapeDtypeStruct((t, SB_HEADS * SB_HEAD_DIM), BF16),
        scratch_shapes=[pltpu.VMEM((tile, LANES), F32), pltpu.VMEM((tile, SB_HEAD_DIM), F32)],
        compiler_params=_params("parallel", "parallel", "arbitrary"),
        name="sb_attention",
    )(q, k, v)


def _mix_out_kernel(x_ref, omla_ref, g_ref, gprev_ref, osb_ref, wdw_ref, bdw_ref, gln_ref, bln_ref,
                    wpw_ref, wout_ref, gpost_ref, o_ref, gext_ref, *, tm, tiles_per_seq):
    i = pl.program_id(0)
    first = (i % tiles_per_seq) == 0
    gext_ref[0:CONV_HALO, :] = jnp.where(first, 0.0, gprev_ref[...])
    gext_ref[CONV_HALO:, :] = g_ref[...]
    base = CONV_HALO - (CONV_WIDTH - 1)
    y = jnp.zeros((tm, CONV_CH), F32) + bdw_ref[...]
    for j in range(CONV_WIDTH):
        y = y + gext_ref[pl.ds(base + j, tm), :] * wdw_ref[j:j + 1, :]
    mu = jnp.mean(y, axis=-1, keepdims=True)
    d = y - mu
    var = jnp.mean(d * d, axis=-1, keepdims=True)
    yn = d * lax.rsqrt(var + EPS) * gln_ref[...] + bln_ref[...]
    act = (yn * jax.nn.sigmoid(yn)).astype(BF16)
    oconv = jnp.dot(act, wpw_ref[...], preferred_element_type=F32).astype(BF16)
    cat = jnp.concatenate([omla_ref[...], oconv, osb_ref[...]], axis=-1)
    mix = jnp.dot(cat, wout_ref[...], preferred_element_type=F32)
    o_ref[...] = x_ref[...] + _rms(mix, gpost_ref[...])


def _mix_out(x, o_mla, g, o_sb, w_dw, b_dw, g_ln, b_ln, w_pw, w_out, g_post, seq, tm=512):
    t = x.shape[0]
    tiles_per_seq = seq // tm
    halo_blocks = tm // CONV_HALO
    row = lambda n: pl.BlockSpec((tm, n), lambda i: (i, 0))
    prev = pl.BlockSpec((CONV_HALO, CONV_CH), lambda i: (jnp.maximum(i * halo_blocks - 1, 0), 0))
    return pl.pallas_call(
        functools.partial(_mix_out_kernel, tm=tm, tiles_per_seq=tiles_per_seq),
        grid=(t // tm,),
        in_specs=[
            row(D_MODEL), row(MLA_HEADS * MLA_V), row(CONV_CH), prev, row(SB_HEADS * SB_HEAD_DIM),
            _resident((CONV_WIDTH, CONV_CH)), _resident((1, CONV_CH)), _resident((1, CONV_CH)),
            _resident((1, CONV_CH)), _resident((CONV_CH, CONV_CH)), _resident((D_MIX, D_MODEL)),
            _resident((1, D_MODEL)),
        ],
        out_specs=row(D_MODEL),
        out_shape=jax.ShapeDtypeStruct((t, D_MODEL), F32),
        scratch_shapes=[pltpu.VMEM((tm + CONV_HALO, CONV_CH), F32)],
        compiler_params=_params("parallel"),
        name="mix_out",
    )(x, o_mla, g, g, o_sb, w_dw, b_dw, g_ln, b_ln, w_pw, w_out, g_post)


def _ple_kernel(x_ref, p_ref, gpre_ref, wg_ref, wp_ref, gpost_ref, o_ref):
    x = x_ref[...]
    h = _rms(x, gpre_ref[...]).astype(BF16)
    gate = jax.nn.sigmoid(jnp.dot(h, wg_ref[...], preferred_element_type=F32))
    e = jnp.dot(p_ref[...].astype(BF16), wp_ref[...], preferred_element_type=F32)
    o_ref[...] = x + _rms(gate * e, gpost_ref[...])


def _ple(x, p, g_pre, w_gate, w_proj, g_post, tm=512):
    t = x.shape[0]
    row = lambda n: pl.BlockSpec((tm, n), lambda i: (i, 0))
    return pl.pallas_call(
        _ple_kernel,
        grid=(t // tm,),
        in_specs=[row(D_MODEL), row(D_PLE), _resident((1, D_MODEL)), _resident((D_MODEL, D_MODEL)),
                  _resident((D_PLE, D_MODEL)), _resident((1, D_MODEL))],
        out_specs=row(D_MODEL),
        out_shape=jax.ShapeDtypeStruct((t, D_MODEL), F32),
        compiler_params=_params("parallel"),
        name="ple",
    )(x, p, g_pre, w_gate, w_proj, g_post)


def _prep_w_in(w_in):
    half = MLA_ROPE // 2
    kr = w_in[:, _U_KR:_U_KR + MLA_ROPE]
    kr_sw = jnp.concatenate([kr[:, half:], kr[:, :half]], axis=1)
    return jnp.concatenate([w_in[:, :_U_KR + MLA_ROPE], kr_sw, w_in[:, _U_KR + MLA_ROPE:]], axis=1).astype(BF16)


def _prep_w_uq(w_uq):
    half = MLA_ROPE // 2
    w = w_uq.reshape(Q_LORA, MLA_HEADS, MLA_QK)
    nope = w[:, :, :MLA_NOPE].reshape(Q_LORA, -1)
    rope = w[:, :, MLA_NOPE:]
    rope_sw = jnp.concatenate([rope[:, :, half:], rope[:, :, :half]], axis=2)
    return jnp.concatenate([nope, rope.reshape(Q_LORA, -1), rope_sw.reshape(Q_LORA, -1)], axis=1).astype(BF16)


def _prep_w_ukv(w_ukv):
    w = w_ukv.reshape(KV_LORA, MLA_HEADS, MLA_NOPE + MLA_V)
    return jnp.concatenate([w[:, :, :MLA_NOPE].reshape(KV_LORA, -1),
                            w[:, :, MLA_NOPE:].reshape(KV_LORA, -1)], axis=1).astype(BF16)


def kernel(x, p, positions, g_ff1_pre, w_ff1_in, w_ff1_out, g_ff1_post, g_mix_pre, w_in, g_cq, w_uq, g_ckv, w_ukv, w_dw, b_dw, g_conv_ln, b_conv_ln, w_pw, w_out, g_mix_post, g_ff2_pre, w_ff2_in, w_ff2_out, g_ff2_post, g_ple_pre, w_ple_gate, w_ple_proj, g_ple_post):
    batch, seq, _ = x.shape
    depth = p.shape[0]
    t = batch * seq
    xt = x.reshape(t, D_MODEL)
    cos_t, sin_t = _rope_tables(positions)
    vec = lambda a: a[:, None, :]
    for i in range(depth):
        xt = _ffn(xt, vec(g_ff1_pre)[i], w_ff1_in[i].astype(BF16), w_ff1_out[i].astype(BF16), vec(g_ff1_post)[i])
        qn, qr, kn, kr, v, g, sq, sk, sv = _proj(
            xt, vec(g_mix_pre)[i], _prep_w_in(w_in[i]), vec(g_cq)[i], _prep_w_uq(w_uq[i]),
            vec(g_ckv)[i], _prep_w_ukv(w_ukv[i]), cos_t, sin_t)
        o_mla = _mla_attention(qn, qr, kn, kr, v, batch, seq)
        o_sb = _sb_attention(sq, sk, sv, batch, seq)
        xt = _mix_out(xt, o_mla, g, o_sb, w_dw[i], vec(b_dw)[i], vec(g_conv_ln)[i], vec(b_conv_ln)[i],
                      w_pw[i].astype(BF16), w_out[i].astype(BF16), vec(g_mix_post)[i], seq)
        xt = _ffn(xt, vec(g_ff2_pre)[i], w_ff2_in[i].astype(BF16), w_ff2_out[i].astype(BF16), vec(g_ff2_post)[i])
        xt = _ple(xt, p[i].reshape(t, D_PLE), vec(g_ple_pre)[i], w_ple_gate[i].astype(BF16),
                  w_ple_proj[i].astype(BF16), vec(g_ple_post)[i])
    return xt.reshape(batch, seq, D_MODEL)
```

```python
import functools

import jax
import jax.numpy as jnp
from jax import lax
from jax.experimental import pallas as pl
from jax.experimental.pallas import tpu as pltpu

F32 = jnp.float32
BF16 = jnp.bfloat16

D_MODEL = 2048
D_PLE = 256
D_FF = 5632
MLA_HEADS = 8
MLA_NOPE = 128
MLA_ROPE = 64
MLA_QK = MLA_NOPE + MLA_ROPE
MLA_V = 128
Q_LORA = 512
KV_LORA = 256
ROPE_THETA = 10000.0
CONV_CH = 512
CONV_WIDTH = 31
SB_HEADS = 4
SB_HEAD_DIM = 128
D_MIX = MLA_HEADS * MLA_V + CONV_CH + SB_HEADS * SB_HEAD_DIM
EPS = 1e-6
NEG = -1e30

LOG2E = 1.4426950408889634
LN2 = 0.6931471805599453
MLA_SCORE_SCALE = MLA_QK ** -0.5 * LOG2E

LANES = 128
MXU_DIM = 256
CONV_HALO = 32

_U_CQ = 0
_U_CKV = Q_LORA
_U_KR = Q_LORA + KV_LORA
_U_CONV = _U_KR + 2 * MLA_ROPE
_U_SB = _U_CONV + 2 * CONV_CH
_U_END = _U_SB + 3 * SB_HEADS * SB_HEAD_DIM

VMEM_LIMIT = 48 * 1024 * 1024


def _params(*sem):
    return pltpu.CompilerParams(dimension_semantics=sem, vmem_limit_bytes=VMEM_LIMIT)


def _resident(shape):
    nd = len(shape)
    return pl.BlockSpec(shape, lambda *_: (0,) * nd, pipeline_mode=pl.Buffered(1))


def _rms(x, g):
    return x * lax.rsqrt(jnp.mean(x * x, axis=-1, keepdims=True) + EPS) * g


def _rope_table_kernel(pos_ref, freq_ref, cos_ref, sin_ref):
    ang = pos_ref[...].astype(F32) * freq_ref[...]
    lane = lax.broadcasted_iota(jnp.int32, ang.shape, 1)
    first_half = (lane & (MLA_ROPE - 1)) < (MLA_ROPE // 2)
    cos_ref[...] = jnp.cos(ang)
    sin_ref[...] = jnp.where(first_half, -jnp.sin(ang), jnp.sin(ang))


def _rope_tables(positions, tm=1024):
    t = positions.size
    half = MLA_ROPE // 2
    inv_freq = ROPE_THETA ** (-jnp.arange(0, MLA_ROPE, 2, dtype=F32) / MLA_ROPE)
    freq = jnp.tile(inv_freq, LANES // half)[None, :]
    pos = positions.reshape(t, 1)
    return pl.pallas_call(
        _rope_table_kernel,
        grid=(t // tm,),
        in_specs=[pl.BlockSpec((tm, 1), lambda i: (i, 0)), _resident((1, LANES))],
        out_specs=[pl.BlockSpec((tm, LANES), lambda i: (i, 0))] * 2,
        out_shape=[jax.ShapeDtypeStruct((t, LANES), F32)] * 2,
        compiler_params=_params("parallel"),
        name="rope_tables",
    )(pos, freq)


def _ffn_kernel(x_ref, gpre_ref, wa_ref, wu_ref, wo_ref, gpost_ref, o_ref, h_ref, *, n_chunk):
    j = pl.program_id(1)

    @pl.when(j == 0)
    def _():
        h_ref[...] = _rms(x_ref[...], gpre_ref[...]).astype(BF16)
        o_ref[...] = jnp.zeros(o_ref.shape, F32)

    h = h_ref[...]
    a = jnp.dot(h, wa_ref[...], preferred_element_type=F32)
    u = jnp.dot(h, wu_ref[...], preferred_element_type=F32)
    act = (a * jax.nn.sigmoid(a) * u).astype(BF16)
    for n in range(0, D_MODEL, n_chunk):
        o_ref[:, n:n + n_chunk] += jnp.dot(act, wo_ref[:, n:n + n_chunk], preferred_element_type=F32)

    @pl.when(j == pl.num_programs(1) - 1)
    def _():
        o_ref[...] = x_ref[...] + 0.5 * _rms(o_ref[...], gpost_ref[...])


def _ffn(x, g_pre, w_in, w_out, g_post, tm=512, tf=512, n_chunk=512):
    t = x.shape[0]
    nf = D_FF // tf
    return pl.pallas_call(
        functools.partial(_ffn_kernel, n_chunk=n_chunk),
        grid=(t // tm, nf),
        in_specs=[
            pl.BlockSpec((tm, D_MODEL), lambda i, j: (i, 0)),
            _resident((1, D_MODEL)),
            pl.BlockSpec((D_MODEL, tf), lambda i, j: (0, j)),
            pl.BlockSpec((D_MODEL, tf), lambda i, j: (0, nf + j)),
            pl.BlockSpec((tf, D_MODEL), lambda i, j: (j, 0)),
            _resident((1, D_MODEL)),
        ],
        out_specs=pl.BlockSpec((tm, D_MODEL), lambda i, j: (i, 0)),
        out_shape=jax.ShapeDtypeStruct((t, D_MODEL), F32),
        scratch_shapes=[pltpu.VMEM((tm, D_MODEL), BF16)],
        compiler_params=_params("parallel", "arbitrary"),
        name="ffn",
    )(x, g_pre, w_in, w_in, w_out, g_post)


def _proj_kernel(x_ref, gpre_ref, win_ref, gcq_ref, wuq_ref, gckv_ref, wukv_ref, cos_ref, sin_ref,
                 qn_ref, qr_ref, kn_ref, kr_ref, v_ref, g_ref, sq_ref, sk_ref, sv_ref):
    h = _rms(x_ref[...], gpre_ref[...]).astype(BF16)
    u = jnp.dot(h, win_ref[...], preferred_element_type=F32)
    cos = cos_ref[...]
    sin = sin_ref[...]
    lane = lax.broadcasted_iota(jnp.int32, cos.shape, 1)
    upper = lane >= MLA_ROPE

    cq = _rms(u[:, _U_CQ:_U_CKV], gcq_ref[...]).astype(BF16)
    q = jnp.dot(cq, wuq_ref[...], preferred_element_type=F32) * MLA_SCORE_SCALE
    n_nope = MLA_HEADS * MLA_NOPE
    n_rope = MLA_HEADS * MLA_ROPE
    qn_ref[...] = q[:, :n_nope].astype(BF16)
    for pair in range(MLA_HEADS // 2):
        lo = n_nope + pair * LANES
        roped = q[:, lo:lo + LANES] * cos + q[:, lo + n_rope:lo + n_rope + LANES] * sin
        qr_ref[:, (2 * pair) * LANES:(2 * pair + 1) * LANES] = jnp.where(upper, 0.0, roped).astype(BF16)
        qr_ref[:, (2 * pair + 1) * LANES:(2 * pair + 2) * LANES] = jnp.where(upper, roped, 0.0).astype(BF16)

    ckv = _rms(u[:, _U_CKV:_U_KR], gckv_ref[...]).astype(BF16)
    kv = jnp.dot(ckv, wukv_ref[...], preferred_element_type=F32)
    kn_ref[...] = kv[:, :n_nope].astype(BF16)
    v_ref[...] = kv[:, n_nope:].astype(BF16)
    t = u[:, _U_KR:_U_CONV] * jnp.where(upper, sin, cos)
    kr_ref[...] = (t + pltpu.roll(t, MLA_ROPE, 1)).astype(BF16)

    a = u[:, _U_CONV:_U_CONV + CONV_CH]
    gate = u[:, _U_CONV + CONV_CH:_U_SB]
    g_ref[...] = a * jax.nn.sigmoid(gate)

    n_sb = SB_HEADS * SB_HEAD_DIM
    sq_ref[...] = (u[:, _U_SB:_U_SB + n_sb] * SB_HEAD_DIM ** -0.5).astype(BF16)
    sk_ref[...] = u[:, _U_SB + n_sb:_U_SB + 2 * n_sb].astype(BF16)
    sv_ref[...] = u[:, _U_SB + 2 * n_sb:_U_END].astype(BF16)


def _proj(x, g_pre, w_in_ext, g_cq, w_uq_ext, g_ckv, w_ukv_ext, cos_t, sin_t, tm=256):
    t = x.shape[0]
    n_nope = MLA_HEADS * MLA_NOPE
    n_sb = SB_HEADS * SB_HEAD_DIM
    row = lambda n: pl.BlockSpec((tm, n), lambda i: (i, 0))
    out_cols = [(n_nope, BF16), (n_nope, BF16), (n_nope, BF16), (LANES, BF16), (n_nope, BF16),
                (CONV_CH, F32), (n_sb, BF16), (n_sb, BF16), (n_sb, BF16)]
    return pl.pallas_call(
        _proj_kernel,
        grid=(t // tm,),
        in_specs=[
            row(D_MODEL), _resident((1, D_MODEL)), _resident(w_in_ext.shape),
            _resident((1, Q_LORA)), _resident(w_uq_ext.shape),
            _resident((1, KV_LORA)), _resident(w_ukv_ext.shape),
            row(LANES), row(LANES),
        ],
        out_specs=[row(n) for n, _ in out_cols],
        out_shape=[jax.ShapeDtypeStruct((t, n), dt) for n, dt in out_cols],
        compiler_params=_params("parallel"),
        name="mixer_proj",
    )(x, g_pre, w_in_ext, g_cq, w_uq_ext, g_ckv, w_ukv_ext, cos_t, sin_t)


def _head(h):
    return slice(h * LANES, (h + 1) * LANES)


def _mla_kernel(qn_ref, qr_ref, kn_ref, kr_ref, v_ref, o_ref, m_ref, l_ref, acc_ref, *, tile, heads):
    qi = pl.program_id(2)
    m_ref[...] = jnp.full(m_ref.shape, NEG, F32)
    l_ref[...] = jnp.zeros(l_ref.shape, F32)
    acc_ref[...] = jnp.zeros(acc_ref.shape, F32)

    def step(kj, diagonal):
        ks = pl.multiple_of(kj * tile, tile)
        kr = kr_ref[pl.ds(ks, tile), :]
        for h in range(heads):
            q = jnp.concatenate([qn_ref[:, _head(h)], qr_ref[:, _head(h)]], axis=-1)
            k = jnp.concatenate([kn_ref[pl.ds(ks, tile), _head(h)], kr], axis=-1)
            s = lax.dot_general(q, k, (((1,), (1,)), ((), ())), preferred_element_type=F32)
            if diagonal:
                row = lax.broadcasted_iota(jnp.int32, s.shape, 0)
                col = lax.broadcasted_iota(jnp.int32, s.shape, 1)
                s = jnp.where(col <= row, s, NEG)
            m_prev = m_ref[h]
            m_new = jnp.maximum(m_prev, jnp.max(s, axis=-1, keepdims=True))
            alpha = jnp.exp2(m_prev - m_new)
            p = jnp.exp2(s - jnp.concatenate([m_new] * (tile // LANES), axis=-1))
            l_ref[h] = alpha * l_ref[h] + jnp.sum(p, axis=-1, keepdims=True)
            pv = jnp.dot(p.astype(BF16), v_ref[pl.ds(ks, tile), _head(h)], preferred_element_type=F32)
            acc_ref[h] = alpha * acc_ref[h] + pv
            m_ref[h] = m_new

    def body(kj, carry):
        step(kj, False)
        return carry

    lax.fori_loop(0, qi, body, 0)
    step(qi, True)
    for h in range(heads):
        o_ref[:, _head(h)] = (acc_ref[h] / l_ref[h]).astype(BF16)


def _mla_attention(qn, qr, kn, kr, v, batch, seq, tile=512, heads=2):
    t = qn.shape[0]
    nq = seq // tile
    qspec = pl.BlockSpec((tile, heads * LANES), lambda b, h, i: (b * nq + i, h))
    kspec = pl.BlockSpec((seq, heads * LANES), lambda b, h, i: (b, h))
    stat = pltpu.VMEM((heads, tile, LANES), F32)
    return pl.pallas_call(
        functools.partial(_mla_kernel, tile=tile, heads=heads),
        grid=(batch, MLA_HEADS // heads, nq),
        in_specs=[qspec, qspec, kspec, pl.BlockSpec((seq, LANES), lambda b, h, i: (b, 0)), kspec],
        out_specs=qspec,
        out_shape=jax.ShapeDtypeStruct((t, MLA_HEADS * MLA_V), BF16),
        scratch_shapes=[stat, stat, stat],
        compiler_params=_params("parallel", "parallel", "arbitrary"),
        name="mla_attention",
    )(qn, qr, kn, kr, v)


def _sb_kernel(q_ref, k_ref, v_ref, o_ref, run_ref, acc_ref, *, tile, heads):
    qi = pl.program_id(2)
    blk = MXU_DIM
    uj = lax.broadcasted_iota(jnp.int32, (2 * blk, blk + LANES), 0) & (blk - 1)
    us = lax.broadcasted_iota(jnp.int32, (2 * blk, blk + LANES), 1)
    suffix = jnp.where((uj > us) | (us >= blk), 1.0, 0.0).astype(BF16)
    run_ref[...] = jnp.zeros(run_ref.shape, F32)
    acc_ref[...] = jnp.zeros(acc_ref.shape, F32)

    def step(kj, diagonal):
        ks = pl.multiple_of(kj * tile, tile)
        for h in range(heads):
            z = lax.dot_general(q_ref[:, _head(h)], k_ref[pl.ds(ks, tile), _head(h)],
                                (((1,), (1,)), ((), ())), preferred_element_type=F32)
            softplus = jnp.log(1.0 + jnp.exp(-jnp.abs(z)))
            log_beta = jnp.minimum(z, 0.0) - softplus
            log_keep = log_beta - z
            if diagonal:
                row = lax.broadcasted_iota(jnp.int32, z.shape, 0)
                col = lax.broadcasted_iota(jnp.int32, z.shape, 1)
                strict = col < row
                log_keep = jnp.where(strict, log_keep, 0.0)
            run = run_ref[h]
            after = [None] * (tile // blk)
            for c in reversed(range(tile // blk)):
                part = log_keep[:, c * blk:(c + 1) * blk]
                hi = part.astype(BF16)
                lo = (part - hi.astype(F32)).astype(BF16)
                sums = jnp.dot(jnp.concatenate([hi, lo], axis=-1), suffix,
                               preferred_element_type=F32)
                after[c] = sums[:, :blk] + jnp.concatenate([run] * (blk // LANES), axis=-1)
                run = run + sums[:, blk:]
            w = jnp.exp(log_beta + jnp.concatenate(after, axis=-1))
            if diagonal:
                w = jnp.where(strict, w, 0.0)
            acc_ref[h] += jnp.dot(w.astype(BF16), v_ref[pl.ds(ks, tile), _head(h)],
                                  preferred_element_type=F32)
            run_ref[h] = run

    step(qi, True)

    def body(i, carry):
        step(qi - 1 - i, False)
        return carry

    lax.fori_loop(0, qi, body, 0)
    for h in range(heads):
        o_ref[:, _head(h)] = acc_ref[h].astype(BF16)


def _sb_attention(q, k, v, batch, seq, tile=512, heads=2):
    t = q.shape[0]
    nq = seq // tile
    qspec = pl.BlockSpec((tile, heads * LANES), lambda b, h, i: (b * nq + i, h))
    kspec = pl.BlockSpec((seq, heads * LANES), lambda b, h, i: (b, h))
    stat = pltpu.VMEM((heads, tile, LANES), F32)
    return pl.pallas_call(
        functools.partial(_sb_kernel, tile=tile, heads=heads),
        grid=(batch, SB_HEADS // heads, nq),
        in_specs=[qspec, kspec, kspec],
        out_specs=qspec,
        out_shape=jax.ShapeDtypeStruct((t, SB_HEADS * SB_HEAD_DIM), BF16),
        scratch_shapes=[stat, stat],
        compiler_params=_params("parallel", "parallel", "arbitrary"),
        name="sb_attention",
    )(q, k, v)


def _mix_out_kernel(x_ref, omla_ref, g_ref, gprev_ref, osb_ref, wdw_ref, bdw_ref, gln_ref, bln_ref,
                    wpw_ref, wout_ref, gpost_ref, o_ref, gext_ref, *, tm, tiles_per_seq):
    i = pl.program_id(0)
    first = (i % tiles_per_seq) == 0
    gext_ref[0:CONV_HALO, :] = jnp.where(first, 0.0, gprev_ref[...])
    gext_ref[CONV_HALO:, :] = g_ref[...]
    base = CONV_HALO - (CONV_WIDTH - 1)
    y = jnp.zeros((tm, CONV_CH), F32) + bdw_ref[...]
    for j in range(CONV_WIDTH):
        y = y + gext_ref[pl.ds(base + j, tm), :] * wdw_ref[j:j + 1, :]
    mu = jnp.mean(y, axis=-1, keepdims=True)
    d = y - mu
    var = jnp.mean(d * d, axis=-1, keepdims=True)
    yn = d * lax.rsqrt(var + EPS) * gln_ref[...] + bln_ref[...]
    act = (yn * jax.nn.sigmoid(yn)).astype(BF16)
    oconv = jnp.dot(act, wpw_ref[...], preferred_element_type=F32).astype(BF16)
    cat = jnp.concatenate([omla_ref[...], oconv, osb_ref[...]], axis=-1)
    mix = jnp.dot(cat, wout_ref[...], preferred_element_type=F32)
    o_ref[...] = x_ref[...] + _rms(mix, gpost_ref[...])


def _mix_out(x, o_mla, g, o_sb, w_dw, b_dw, g_ln, b_ln, w_pw, w_out, g_post, seq, tm=512):
    t = x.shape[0]
    tiles_per_seq = seq // tm
    halo_blocks = tm // CONV_HALO
    row = lambda n: pl.BlockSpec((tm, n), lambda i: (i, 0))
    prev = pl.BlockSpec((CONV_HALO, CONV_CH), lambda i: (jnp.maximum(i * halo_blocks - 1, 0), 0))
    return pl.pallas_call(
        functools.partial(_mix_out_kernel, tm=tm, tiles_per_seq=tiles_per_seq),
        grid=(t // tm,),
        in_specs=[
            row(D_MODEL), row(MLA_HEADS * MLA_V), row(CONV_CH), prev, row(SB_HEADS * SB_HEAD_DIM),
            _resident((CONV_WIDTH, CONV_CH)), _resident((1, CONV_CH)), _resident((1, CONV_CH)),
            _resident((1, CONV_CH)), _resident((CONV_CH, CONV_CH)), _resident((D_MIX, D_MODEL)),
            _resident((1, D_MODEL)),
        ],
        out_specs=row(D_MODEL),
        out_shape=jax.ShapeDtypeStruct((t, D_MODEL), F32),
        scratch_shapes=[pltpu.VMEM((tm + CONV_HALO, CONV_CH), F32)],
        compiler_params=_params("parallel"),
        name="mix_out",
    )(x, o_mla, g, g, o_sb, w_dw, b_dw, g_ln, b_ln, w_pw, w_out, g_post)


def _ple_kernel(x_ref, p_ref, gpre_ref, wg_ref, wp_ref, gpost_ref, o_ref):
    x = x_ref[...]
    h = _rms(x, gpre_ref[...]).astype(BF16)
    gate = jax.nn.sigmoid(jnp.dot(h, wg_ref[...], preferred_element_type=F32))
    e = jnp.dot(p_ref[...].astype(BF16), wp_ref[...], preferred_element_type=F32)
    o_ref[...] = x + _rms(gate * e, gpost_ref[...])


def _ple(x, p, g_pre, w_gate, w_proj, g_post, tm=512):
    t = x.shape[0]
    row = lambda n: pl.BlockSpec((tm, n), lambda i: (i, 0))
    return pl.pallas_call(
        _ple_kernel,
        grid=(t // tm,),
        in_specs=[row(D_MODEL), row(D_PLE), _resident((1, D_MODEL)), _resident((D_MODEL, D_MODEL)),
                  _resident((D_PLE, D_MODEL)), _resident((1, D_MODEL))],
        out_specs=row(D_MODEL),
        out_shape=jax.ShapeDtypeStruct((t, D_MODEL), F32),
        compiler_params=_params("parallel"),
        name="ple",
    )(x, p, g_pre, w_gate, w_proj, g_post)


def _prep_w_in(w_in):
    half = MLA_ROPE // 2
    kr = w_in[:, _U_KR:_U_KR + MLA_ROPE]
    kr_sw = jnp.concatenate([kr[:, half:], kr[:, :half]], axis=1)
    return jnp.concatenate([w_in[:, :_U_KR + MLA_ROPE], kr_sw, w_in[:, _U_KR + MLA_ROPE:]], axis=1).astype(BF16)


def _prep_w_uq(w_uq):
    half = MLA_ROPE // 2
    w = w_uq.reshape(Q_LORA, MLA_HEADS, MLA_QK)
    nope = w[:, :, :MLA_NOPE].reshape(Q_LORA, -1)
    rope = w[:, :, MLA_NOPE:]
    rope_sw = jnp.concatenate([rope[:, :, half:], rope[:, :, :half]], axis=2)
    return jnp.concatenate([nope, rope.reshape(Q_LORA, -1), rope_sw.reshape(Q_LORA, -1)], axis=1).astype(BF16)


def _prep_w_ukv(w_ukv):
    w = w_ukv.reshape(KV_LORA, MLA_HEADS, MLA_NOPE + MLA_V)
    return jnp.concatenate([w[:, :, :MLA_NOPE].reshape(KV_LORA, -1),
                            w[:, :, MLA_NOPE:].reshape(KV_LORA, -1)], axis=1).astype(BF16)


def kernel(x, p, positions, g_ff1_pre, w_ff1_in, w_ff1_out, g_ff1_post, g_mix_pre, w_in, g_cq, w_uq, g_ckv, w_ukv, w_dw, b_dw, g_conv_ln, b_conv_ln, w_pw, w_out, g_mix_post, g_ff2_pre, w_ff2_in, w_ff2_out, g_ff2_post, g_ple_pre, w_ple_gate, w_ple_proj, g_ple_post):
    batch, seq, _ = x.shape
    depth = p.shape[0]
    t = batch * seq
    xt = x.reshape(t, D_MODEL)
    cos_t, sin_t = _rope_tables(positions)
    vec = lambda a: a[:, None, :]
    for i in range(depth):
        xt = _ffn(xt, vec(g_ff1_pre)[i], w_ff1_in[i].astype(BF16), w_ff1_out[i].astype(BF16), vec(g_ff1_post)[i])
        qn, qr, kn, kr, v, g, sq, sk, sv = _proj(
            xt, vec(g_mix_pre)[i], _prep_w_in(w_in[i]), vec(g_cq)[i], _prep_w_uq(w_uq[i]),
            vec(g_ckv)[i], _prep_w_ukv(w_ukv[i]), cos_t, sin_t)
        o_mla = _mla_attention(qn, qr, kn, kr, v, batch, seq)
        o_sb = _sb_attention(sq, sk, sv, batch, seq)
        xt = _mix_out(xt, o_mla, g, o_sb, w_dw[i], vec(b_dw)[i], vec(g_conv_ln)[i], vec(b_conv_ln)[i],
                      w_pw[i].astype(BF16), w_out[i].astype(BF16), vec(g_mix_post)[i], seq)
        xt = _ffn(xt, vec(g_ff2_pre)[i], w_ff2_in[i].astype(BF16), w_ff2_out[i].astype(BF16), vec(g_ff2_post)[i])
        xt = _ple(xt, p[i].reshape(t, D_PLE), vec(g_ple_pre)[i], w_ple_gate[i].astype(BF16),
                  w_ple_proj[i].astype(BF16), vec(g_ple_post)[i])
    return xt.reshape(batch, seq, D_MODEL)
```

```python
import functools

import jax
import jax.numpy as jnp
from jax import lax
from jax.experimental import pallas as pl
from jax.experimental.pallas import tpu as pltpu

F32 = jnp.float32
BF16 = jnp.bfloat16

D_MODEL = 2048
D_PLE = 256
D_FF = 5632
MLA_HEADS = 8
MLA_NOPE = 128
MLA_ROPE = 64
MLA_QK = MLA_NOPE + MLA_ROPE
MLA_V = 128
Q_LORA = 512
KV_LORA = 256
ROPE_THETA = 10000.0
CONV_CH = 512
CONV_WIDTH = 31
SB_HEADS = 4
SB_HEAD_DIM = 128
D_MIX = MLA_HEADS * MLA_V + CONV_CH + SB_HEADS * SB_HEAD_DIM
EPS = 1e-6
NEG = -1e30

LOG2E = 1.4426950408889634
LN2 = 0.6931471805599453
MLA_SCORE_SCALE = MLA_QK ** -0.5 * LOG2E

LANES = 128
SUBLANES = 8
MXU_DIM = 256
CONV_HALO = 32

_U_CQ = 0
_U_CKV = Q_LORA
_U_KR = Q_LORA + KV_LORA
_U_CONV = _U_KR + 2 * MLA_ROPE
_U_SB = _U_CONV + 2 * CONV_CH
_U_END = _U_SB + 3 * SB_HEADS * SB_HEAD_DIM

VMEM_LIMIT = 48 * 1024 * 1024
FFN_VMEM_LIMIT = 58 * 1024 * 1024


def _params(*sem, vmem=VMEM_LIMIT):
    return pltpu.CompilerParams(dimension_semantics=sem, vmem_limit_bytes=vmem)


def _resident(shape):
    nd = len(shape)
    return pl.BlockSpec(shape, lambda *_: (0,) * nd, pipeline_mode=pl.Buffered(1))


def _layer_vec(n, layer):
    return pl.BlockSpec((None, 1, n), lambda *_: (layer, 0, 0), pipeline_mode=pl.Buffered(1))


def _layer_mat(shape, layer):
    return pl.BlockSpec((None,) + tuple(shape), lambda *_: (layer, 0, 0), pipeline_mode=pl.Buffered(1))


def _rms(x, g):
    return x * lax.rsqrt(jnp.mean(x * x, axis=-1, keepdims=True) + EPS) * g


def _rope_table_kernel(pos_ref, freq_ref, cos_ref, sin_ref):
    ang = pos_ref[...].astype(F32) * freq_ref[...]
    lane = lax.broadcasted_iota(jnp.int32, ang.shape, 1)
    first_half = (lane & (MLA_ROPE - 1)) < (MLA_ROPE // 2)
    cos_ref[...] = jnp.cos(ang)
    sin_ref[...] = jnp.where(first_half, -jnp.sin(ang), jnp.sin(ang))


def _rope_tables(positions, tm=1024):
    t = positions.size
    half = MLA_ROPE // 2
    inv_freq = ROPE_THETA ** (-jnp.arange(0, MLA_ROPE, 2, dtype=F32) / MLA_ROPE)
    freq = jnp.tile(inv_freq, LANES // half)[None, :]
    pos = positions.reshape(t, 1)
    return pl.pallas_call(
        _rope_table_kernel,
        grid=(t // tm,),
        in_specs=[pl.BlockSpec((tm, 1), lambda i: (i, 0)), _resident((1, LANES))],
        out_specs=[pl.BlockSpec((tm, LANES), lambda i: (i, 0))] * 2,
        out_shape=[jax.ShapeDtypeStruct((t, LANES), F32)] * 2,
        compiler_params=_params("parallel"),
        name="rope_tables",
    )(pos, freq)


def _ffn_kernel(x_ref, gpre_ref, wa_ref, wu_ref, wo_ref, gpost_ref, o_ref, h_ref, *, n_chunk):
    j = pl.program_id(1)

    @pl.when(j == 0)
    def _():
        h_ref[...] = _rms(x_ref[...], gpre_ref[...]).astype(BF16)
        o_ref[...] = jnp.zeros(o_ref.shape, F32)

    h = h_ref[...]
    a = jnp.dot(h, wa_ref[...].astype(BF16), preferred_element_type=F32)
    u = jnp.dot(h, wu_ref[...].astype(BF16), preferred_element_type=F32)
    act = (a * jax.nn.sigmoid(a) * u).astype(BF16)
    for n in range(0, D_MODEL, n_chunk):
        o_ref[:, n:n + n_chunk] += jnp.dot(act, wo_ref[:, n:n + n_chunk].astype(BF16),
                                           preferred_element_type=F32)

    @pl.when(j == pl.num_programs(1) - 1)
    def _():
        o_ref[...] = x_ref[...] + 0.5 * _rms(o_ref[...], gpost_ref[...])


def _ffn(x, g_pre, w_in, w_out, g_post, layer, tm=1024, tf=256, n_chunk=512):
    t = x.shape[0]
    nf = D_FF // tf
    return pl.pallas_call(
        functools.partial(_ffn_kernel, n_chunk=n_chunk),
        grid=(t // tm, nf),
        in_specs=[
            pl.BlockSpec((tm, D_MODEL), lambda i, j: (i, 0)),
            _layer_vec(D_MODEL, layer),
            pl.BlockSpec((None, D_MODEL, tf), lambda i, j: (layer, 0, j)),
            pl.BlockSpec((None, D_MODEL, tf), lambda i, j: (layer, 0, nf + j)),
            pl.BlockSpec((None, tf, D_MODEL), lambda i, j: (layer, j, 0)),
            _layer_vec(D_MODEL, layer),
        ],
        out_specs=pl.BlockSpec((tm, D_MODEL), lambda i, j: (i, 0)),
        out_shape=jax.ShapeDtypeStruct((t, D_MODEL), F32),
        scratch_shapes=[pltpu.VMEM((tm, D_MODEL), BF16)],
        compiler_params=_params("parallel", "arbitrary", vmem=FFN_VMEM_LIMIT),
        name="ffn",
    )(x, g_pre, w_in, w_in, w_out, g_post)


def _proj_kernel(x_ref, gpre_ref, win_ref, gcq_ref, wuq_ref, gckv_ref, wukv_ref, cos_ref, sin_ref,
                 qn_ref, qr_ref, kn_ref, kr_ref, v_ref, g_ref, sq_ref, sk_ref, sv_ref):
    h = _rms(x_ref[...], gpre_ref[...]).astype(BF16)
    u = jnp.dot(h, win_ref[...], preferred_element_type=F32)
    cos = cos_ref[...]
    sin = sin_ref[...]
    lane = lax.broadcasted_iota(jnp.int32, cos.shape, 1)
    upper = lane >= MLA_ROPE

    cq = _rms(u[:, _U_CQ:_U_CKV], gcq_ref[...]).astype(BF16)
    q = jnp.dot(cq, wuq_ref[...], preferred_element_type=F32) * MLA_SCORE_SCALE
    n_nope = MLA_HEADS * MLA_NOPE
    n_rope = MLA_HEADS * MLA_ROPE
    qn_ref[...] = q[:, :n_nope].astype(BF16)
    for pair in range(MLA_HEADS // 2):
        lo = n_nope + pair * LANES
        roped = q[:, lo:lo + LANES] * cos + q[:, lo + n_rope:lo + n_rope + LANES] * sin
        qr_ref[:, (2 * pair) * LANES:(2 * pair + 1) * LANES] = jnp.where(upper, 0.0, roped).astype(BF16)
        qr_ref[:, (2 * pair + 1) * LANES:(2 * pair + 2) * LANES] = jnp.where(upper, roped, 0.0).astype(BF16)

    ckv = _rms(u[:, _U_CKV:_U_KR], gckv_ref[...]).astype(BF16)
    kv = jnp.dot(ckv, wukv_ref[...], preferred_element_type=F32)
    kn_ref[...] = kv[:, :n_nope].astype(BF16)
    v_ref[...] = kv[:, n_nope:].astype(BF16)
    t = u[:, _U_KR:_U_CONV] * jnp.where(upper, sin, cos)
    kr_ref[...] = (t + pltpu.roll(t, MLA_ROPE, 1)).astype(BF16)

    a = u[:, _U_CONV:_U_CONV + CONV_CH]
    gate = u[:, _U_CONV + CONV_CH:_U_SB]
    g_ref[...] = a * jax.nn.sigmoid(gate)

    n_sb = SB_HEADS * SB_HEAD_DIM
    sq_ref[...] = (u[:, _U_SB:_U_SB + n_sb] * SB_HEAD_DIM ** -0.5).astype(BF16)
    sk_ref[...] = u[:, _U_SB + n_sb:_U_SB + 2 * n_sb].astype(BF16)
    sv_ref[...] = u[:, _U_SB + 2 * n_sb:_U_END].astype(BF16)


def _proj(x, g_pre, w_in_ext, g_cq, w_uq_ext, g_ckv, w_ukv_ext, cos_t, sin_t, layer, tm=256):
    t = x.shape[0]
    n_nope = MLA_HEADS * MLA_NOPE
    n_sb = SB_HEADS * SB_HEAD_DIM
    row = lambda n: pl.BlockSpec((tm, n), lambda i: (i, 0))
    out_cols = [(n_nope, BF16), (n_nope, BF16), (n_nope, BF16), (LANES, BF16), (n_nope, BF16),
                (CONV_CH, F32), (n_sb, BF16), (n_sb, BF16), (n_sb, BF16)]
    return pl.pallas_call(
        _proj_kernel,
        grid=(t // tm,),
        in_specs=[
            row(D_MODEL), _layer_vec(D_MODEL, layer), _layer_mat(w_in_ext.shape[1:], layer),
            _layer_vec(Q_LORA, layer), _layer_mat(w_uq_ext.shape[1:], layer),
            _layer_vec(KV_LORA, layer), _layer_mat(w_ukv_ext.shape[1:], layer),
            row(LANES), row(LANES),
        ],
        out_specs=[row(n) for n, _ in out_cols],
        out_shape=[jax.ShapeDtypeStruct((t, n), dt) for n, dt in out_cols],
        compiler_params=_params("parallel"),
        name="mixer_proj",
    )(x, g_pre, w_in_ext, g_cq, w_uq_ext, g_ckv, w_ukv_ext, cos_t, sin_t)


def _head(h):
    return slice(h * LANES, (h + 1) * LANES)


def _mla_kernel(qn_ref, qr_ref, kn_ref, kr_ref, v_ref, o_ref, m_ref, l_ref, acc_ref, *, tile, heads):
    qi = pl.program_id(2)
    m_ref[...] = jnp.full(m_ref.shape, NEG, F32)
    l_ref[...] = jnp.zeros(l_ref.shape, F32)
    acc_ref[...] = jnp.zeros(acc_ref.shape, F32)

    def step(kj, diagonal):
        ks = pl.multiple_of(kj * tile, tile)
        kr = kr_ref[pl.ds(ks, tile), :]
        for h in range(heads):
            q = jnp.concatenate([qn_ref[:, _head(h)], qr_ref[:, _head(h)]], axis=-1)
            k = jnp.concatenate([kn_ref[pl.ds(ks, tile), _head(h)], kr], axis=-1)
            s = lax.dot_general(q, k, (((1,), (1,)), ((), ())), preferred_element_type=F32)
            if diagonal:
                row = lax.broadcasted_iota(jnp.int32, s.shape, 0)
                col = lax.broadcasted_iota(jnp.int32, s.shape, 1)
                s = jnp.where(col <= row, s, NEG)
            m_prev = m_ref[h]
            m_new = jnp.maximum(m_prev, jnp.max(s, axis=-1, keepdims=True))
            alpha = jnp.exp2(m_prev - m_new)
            p = jnp.exp2(s - jnp.concatenate([m_new] * (tile // LANES), axis=-1))
            l_ref[h] = alpha * l_ref[h] + jnp.sum(p, axis=-1, keepdims=True)
            pv = jnp.dot(p.astype(BF16), v_ref[pl.ds(ks, tile), _head(h)], preferred_element_type=F32)
            acc_ref[h] = alpha * acc_ref[h] + pv
            m_ref[h] = m_new

    def body(kj, carry):
        step(kj, False)
        return carry

    lax.fori_loop(0, qi, body, 0)
    step(qi, True)
    for h in range(heads):
        o_ref[:, _head(h)] = (acc_ref[h] / l_ref[h]).astype(BF16)


def _mla_attention(qn, qr, kn, kr, v, batch, seq, tile=512, heads=4):
    t = qn.shape[0]
    nq = seq // tile
    qspec = pl.BlockSpec((tile, heads * LANES), lambda b, h, i: (b * nq + i, h))
    kspec = pl.BlockSpec((seq, heads * LANES), lambda b, h, i: (b, h))
    return pl.pallas_call(
        functools.partial(_mla_kernel, tile=tile, heads=heads),
        grid=(batch, MLA_HEADS // heads, nq),
        in_specs=[qspec, qspec, kspec, pl.BlockSpec((seq, LANES), lambda b, h, i: (b, 0)), kspec],
        out_specs=qspec,
        out_shape=jax.ShapeDtypeStruct((t, MLA_HEADS * MLA_V), BF16),
        scratch_shapes=[pltpu.VMEM((heads, tile, LANES), F32)] * 3,
        compiler_params=_params("parallel", "parallel", "arbitrary"),
        name="mla_attention",
    )(qn, qr, kn, kr, v)


def _sb_kernel(q_ref, k_ref, v_ref, o_ref, run_ref, acc_ref, *, tile, heads):
    qi = pl.program_id(2)
    blk = MXU_DIM
    uj = lax.broadcasted_iota(jnp.int32, (2 * blk, blk), 0) & (blk - 1)
    us = lax.broadcasted_iota(jnp.int32, (2 * blk, blk), 1)
    suffix = jnp.where(uj > us, 1.0, 0.0).astype(BF16)
    run_ref[...] = jnp.zeros(run_ref.shape, F32)
    acc_ref[...] = jnp.zeros(acc_ref.shape, F32)

    def step(kj, diagonal):
        ks = pl.multiple_of(kj * tile, tile)
        for h in range(heads):
            z = lax.dot_general(q_ref[:, _head(h)], k_ref[pl.ds(ks, tile), _head(h)],
                                (((1,), (1,)), ((), ())), preferred_element_type=F32)
            softplus = jnp.log(1.0 + jnp.exp(-jnp.abs(z)))
            log_beta = jnp.minimum(z, 0.0) - softplus
            log_keep = log_beta - z
            if diagonal:
                row = lax.broadcasted_iota(jnp.int32, z.shape, 0)
                col = lax.broadcasted_iota(jnp.int32, z.shape, 1)
                strict = col < row
                log_keep = jnp.where(strict, log_keep, 0.0)
            run = run_ref[h]
            after = [None] * (tile // blk)
            for c in reversed(range(tile // blk)):
                part = log_keep[:, c * blk:(c + 1) * blk]
                hi = part.astype(BF16)
                lo = (part - hi.astype(F32)).astype(BF16)
                sums = jnp.dot(jnp.concatenate([hi, lo], axis=-1), suffix,
                               preferred_element_type=F32)
                after[c] = sums + jnp.concatenate([run] * (blk // LANES), axis=-1)
                run = run + (sums[:, 0:1] + part[:, 0:1])
            w = jnp.exp(log_beta + jnp.concatenate(after, axis=-1))
            if diagonal:
                w = jnp.where(strict, w, 0.0)
            acc_ref[h] += jnp.dot(w.astype(BF16), v_ref[pl.ds(ks, tile), _head(h)],
                                  preferred_element_type=F32)
            run_ref[h] = run

    step(qi, True)

    def body(i, carry):
        step(qi - 1 - i, False)
        return carry

    lax.fori_loop(0, qi, body, 0)
    for h in range(heads):
        o_ref[:, _head(h)] = acc_ref[h].astype(BF16)


def _sb_attention(q, k, v, batch, seq, tile=512, heads=2):
    t = q.shape[0]
    nq = seq // tile
    qspec = pl.BlockSpec((tile, heads * LANES), lambda b, h, i: (b * nq + i, h))
    kspec = pl.BlockSpec((seq, heads * LANES), lambda b, h, i: (b, h))
    stat = pltpu.VMEM((heads, tile, LANES), F32)
    return pl.pallas_call(
        functools.partial(_sb_kernel, tile=tile, heads=heads),
        grid=(batch, SB_HEADS // heads, nq),
        in_specs=[qspec, kspec, kspec],
        out_specs=qspec,
        out_shape=jax.ShapeDtypeStruct((t, SB_HEADS * SB_HEAD_DIM), BF16),
        scratch_shapes=[stat, stat],
        compiler_params=_params("parallel", "parallel", "arbitrary"),
        name="sb_attention",
    )(q, k, v)


def _mix_out_kernel(x_ref, omla_ref, g_ref, gprev_ref, osb_ref, wdw_ref, bdw_ref, gln_ref, bln_ref,
                    wpw_ref, wout_ref, gpost_ref, o_ref, gext_ref, shift_ref, *, tm, tiles_per_seq):
    i = pl.program_id(0)
    n_mla = MLA_HEADS * MLA_V
    first = (i % tiles_per_seq) == 0
    gext_ref[0:CONV_HALO, :] = jnp.where(first, 0.0, gprev_ref[...])
    gext_ref[CONV_HALO:, :] = g_ref[...]
    base = CONV_HALO - (CONV_WIDTH - 1)
    phase_offs = [[o for o in range(base, base + CONV_WIDTH) if o % SUBLANES == ph] for ph in range(SUBLANES)]
    mix = (jnp.dot(omla_ref[...], wout_ref[:n_mla, :], preferred_element_type=F32)
           + jnp.dot(osb_ref[...], wout_ref[n_mla + CONV_CH:, :], preferred_element_type=F32))
    y = jnp.zeros((tm, CONV_CH), F32) + bdw_ref[...]
    for phase, offs in enumerate(phase_offs):
        rows = tm + offs[-1] - phase
        shift_ref[phase, 0:rows, :] = gext_ref[pl.ds(phase, rows), :]
        for o in offs:
            y = y + shift_ref[phase, o - phase:o - phase + tm, :] * wdw_ref[o - base:o - base + 1, :]
    mu = jnp.mean(y, axis=-1, keepdims=True)
    d = y - mu
    var = jnp.mean(d * d, axis=-1, keepdims=True)
    yn = d * lax.rsqrt(var + EPS) * gln_ref[...] + bln_ref[...]
    act = (yn * jax.nn.sigmoid(yn)).astype(BF16)
    oconv = jnp.dot(act, wpw_ref[...], preferred_element_type=F32).astype(BF16)
    mix = mix + jnp.dot(oconv, wout_ref[n_mla:n_mla + CONV_CH, :], preferred_element_type=F32)
    o_ref[...] = x_ref[...] + _rms(mix, gpost_ref[...])


def _mix_out(x, o_mla, g, o_sb, w_dw, b_dw, g_ln, b_ln, w_pw, w_out, g_post, seq, layer, tm=512):
    t = x.shape[0]
    tiles_per_seq = seq // tm
    halo_blocks = tm // CONV_HALO
    row = lambda n: pl.BlockSpec((tm, n), lambda i: (i, 0))
    prev = pl.BlockSpec((CONV_HALO, CONV_CH), lambda i: (jnp.maximum(i * halo_blocks - 1, 0), 0))
    return pl.pallas_call(
        functools.partial(_mix_out_kernel, tm=tm, tiles_per_seq=tiles_per_seq),
        grid=(t // tm,),
        in_specs=[
            row(D_MODEL), row(MLA_HEADS * MLA_V), row(CONV_CH), prev, row(SB_HEADS * SB_HEAD_DIM),
            _layer_mat((CONV_WIDTH, CONV_CH), layer), _layer_vec(CONV_CH, layer), _layer_vec(CONV_CH, layer),
            _layer_vec(CONV_CH, layer), _layer_mat((CONV_CH, CONV_CH), layer),
            _layer_mat((D_MIX, D_MODEL), layer), _layer_vec(D_MODEL, layer),
        ],
        out_specs=row(D_MODEL),
        out_shape=jax.ShapeDtypeStruct((t, D_MODEL), F32),
        scratch_shapes=[pltpu.VMEM((tm + CONV_HALO, CONV_CH), F32),
                        pltpu.VMEM((SUBLANES, tm + CONV_HALO, CONV_CH), F32)],
        compiler_params=_params("parallel"),
        name="mix_out",
    )(x, o_mla, g, g, o_sb, w_dw, b_dw, g_ln, b_ln, w_pw, w_out, g_post)


def _ple_kernel(x_ref, p_ref, gpre_ref, wg_ref, wp_ref, gpost_ref, o_ref):
    x = x_ref[...]
    h = _rms(x, gpre_ref[...]).astype(BF16)
    gate = jax.nn.sigmoid(jnp.dot(h, wg_ref[...], preferred_element_type=F32))
    e = jnp.dot(p_ref[...].astype(BF16), wp_ref[...], preferred_element_type=F32)
    o_ref[...] = x + _rms(gate * e, gpost_ref[...])


def _ple(x, p, g_pre, w_gate, w_proj, g_post, layer, tm=512):
    t = x.shape[0]
    row = lambda n: pl.BlockSpec((tm, n), lambda i: (i, 0))
    return pl.pallas_call(
        _ple_kernel,
        grid=(t // tm,),
        in_specs=[row(D_MODEL), pl.BlockSpec((None, tm, D_PLE), lambda i: (layer, i, 0)),
                  _layer_vec(D_MODEL, layer), _layer_mat((D_MODEL, D_MODEL), layer),
                  _layer_mat((D_PLE, D_MODEL), layer), _layer_vec(D_MODEL, layer)],
        out_specs=row(D_MODEL),
        out_shape=jax.ShapeDtypeStruct((t, D_MODEL), F32),
        compiler_params=_params("parallel"),
        name="ple",
    )(x, p, g_pre, w_gate, w_proj, g_post)


def _prep_w_in(w_in):
    half = MLA_ROPE // 2
    kr = w_in[:, :, _U_KR:_U_KR + MLA_ROPE]
    kr_sw = jnp.concatenate([kr[:, :, half:], kr[:, :, :half]], axis=2)
    return jnp.concatenate([w_in[:, :, :_U_KR + MLA_ROPE], kr_sw, w_in[:, :, _U_KR + MLA_ROPE:]],
                           axis=2).astype(BF16)


def _prep_w_uq(w_uq):
    depth = w_uq.shape[0]
    half = MLA_ROPE // 2
    w = w_uq.reshape(depth, Q_LORA, MLA_HEADS, MLA_QK)
    nope = w[..., :MLA_NOPE].reshape(depth, Q_LORA, -1)
    rope = w[..., MLA_NOPE:]
    rope_sw = jnp.concatenate([rope[..., half:], rope[..., :half]], axis=-1)
    return jnp.concatenate([nope, rope.reshape(depth, Q_LORA, -1), rope_sw.reshape(depth, Q_LORA, -1)],
                           axis=2).astype(BF16)


def _prep_w_ukv(w_ukv):
    depth = w_ukv.shape[0]
    w = w_ukv.reshape(depth, KV_LORA, MLA_HEADS, MLA_NOPE + MLA_V)
    return jnp.concatenate([w[..., :MLA_NOPE].reshape(depth, KV_LORA, -1),
                            w[..., MLA_NOPE:].reshape(depth, KV_LORA, -1)], axis=2).astype(BF16)


def kernel(x, p, positions, g_ff1_pre, w_ff1_in, w_ff1_out, g_ff1_post, g_mix_pre, w_in, g_cq, w_uq, g_ckv, w_ukv, w_dw, b_dw, g_conv_ln, b_conv_ln, w_pw, w_out, g_mix_post, g_ff2_pre, w_ff2_in, w_ff2_out, g_ff2_post, g_ple_pre, w_ple_gate, w_ple_proj, g_ple_post):
    batch, seq, _ = x.shape
    depth = p.shape[0]
    t = batch * seq
    xt = x.reshape(t, D_MODEL)
    pt = p.reshape(depth, t, D_PLE)
    cos_t, sin_t = _rope_tables(positions)
    vec = lambda a: a[:, None, :]
    w_in_ext, w_uq_ext, w_ukv_ext = _prep_w_in(w_in), _prep_w_uq(w_uq), _prep_w_ukv(w_ukv)
    w_pw_b, w_out_b = w_pw.astype(BF16), w_out.astype(BF16)
    w_gate_b, w_proj_b = w_ple_gate.astype(BF16), w_ple_proj.astype(BF16)
    for i in range(depth):
        xt = _ffn(xt, vec(g_ff1_pre), w_ff1_in, w_ff1_out, vec(g_ff1_post), i)
        qn, qr, kn, kr, v, g, sq, sk, sv = _proj(
            xt, vec(g_mix_pre), w_in_ext, vec(g_cq), w_uq_ext, vec(g_ckv), w_ukv_ext, cos_t, sin_t, i)
        o_mla = _mla_attention(qn, qr, kn, kr, v, batch, seq)
        o_sb = _sb_attention(sq, sk, sv, batch, seq)
        xt = _mix_out(xt, o_mla, g, o_sb, w_dw, vec(b_dw), vec(g_conv_ln), vec(b_conv_ln),
                      w_pw_b, w_out_b, vec(g_mix_post), seq, i)
        xt = _ffn(xt, vec(g_ff2_pre), w_ff2_in, w_ff2_out, vec(g_ff2_post), i)
        xt = _ple(xt, pt, vec(g_ple_pre), w_gate_b, w_proj_b, vec(g_ple_post), i)
    return xt.reshape(batch, seq, D_MODEL)
```

```python
import functools

import jax
import jax.numpy as jnp
from jax import lax
from jax.experimental import pallas as pl
from jax.experimental.pallas import tpu as pltpu

F32 = jnp.float32
BF16 = jnp.bfloat16

D_MODEL = 2048
D_PLE = 256
D_FF = 5632
MLA_HEADS = 8
MLA_NOPE = 128
MLA_ROPE = 64
MLA_QK = MLA_NOPE + MLA_ROPE
MLA_V = 128
Q_LORA = 512
KV_LORA = 256
ROPE_THETA = 10000.0
CONV_CH = 512
CONV_WIDTH = 31
SB_HEADS = 4
SB_HEAD_DIM = 128
D_MIX = MLA_HEADS * MLA_V + CONV_CH + SB_HEADS * SB_HEAD_DIM
EPS = 1e-6
NEG = -1e30

LOG2E = 1.4426950408889634
LN2 = 0.6931471805599453
MLA_SCORE_SCALE = MLA_QK ** -0.5 * LOG2E

LANES = 128
SUBLANES = 8
MXU_DIM = 256
NORM_ROWS = 128
CONV_HALO = 32

_U_CQ = 0
_U_CKV = Q_LORA
_U_KR = Q_LORA + KV_LORA
_U_CONV = _U_KR + 2 * MLA_ROPE
_U_SB = _U_CONV + 2 * CONV_CH
_U_END = _U_SB + 3 * SB_HEADS * SB_HEAD_DIM

VMEM_LIMIT = 48 * 1024 * 1024
FFN_VMEM_LIMIT = 58 * 1024 * 1024


def _params(*sem, vmem=VMEM_LIMIT):
    return pltpu.CompilerParams(dimension_semantics=sem, vmem_limit_bytes=vmem)


def _resident(shape):
    nd = len(shape)
    return pl.BlockSpec(shape, lambda *_: (0,) * nd, pipeline_mode=pl.Buffered(1))


def _layer_vec(n, layer):
    return pl.BlockSpec((None, 1, n), lambda *_: (layer, 0, 0), pipeline_mode=pl.Buffered(1))


def _layer_mat(shape, layer):
    return pl.BlockSpec((None,) + tuple(shape), lambda *_: (layer, 0, 0), pipeline_mode=pl.Buffered(1))


def _rms(x, g):
    return x * lax.rsqrt(jnp.mean(x * x, axis=-1, keepdims=True) + EPS) * g


def _rope_table_kernel(pos_ref, freq_ref, cos_ref, sin_ref):
    ang = pos_ref[...].astype(F32) * freq_ref[...]
    lane = lax.broadcasted_iota(jnp.int32, ang.shape, 1)
    first_half = (lane & (MLA_ROPE - 1)) < (MLA_ROPE // 2)
    cos_ref[...] = jnp.cos(ang)
    sin_ref[...] = jnp.where(first_half, -jnp.sin(ang), jnp.sin(ang))


def _rope_tables(positions, tm=1024):
    t = positions.size
    half = MLA_ROPE // 2
    inv_freq = ROPE_THETA ** (-jnp.arange(0, MLA_ROPE, 2, dtype=F32) / MLA_ROPE)
    freq = jnp.tile(inv_freq, LANES // half)[None, :]
    pos = positions.reshape(t, 1)
    return pl.pallas_call(
        _rope_table_kernel,
        grid=(t // tm,),
        in_specs=[pl.BlockSpec((tm, 1), lambda i: (i, 0)), _resident((1, LANES))],
        out_specs=[pl.BlockSpec((tm, LANES), lambda i: (i, 0))] * 2,
        out_shape=[jax.ShapeDtypeStruct((t, LANES), F32)] * 2,
        compiler_params=_params("parallel"),
        name="rope_tables",
    )(pos, freq)


def _ffn_kernel(x_ref, gpre_ref, wa_ref, wu_ref, wo_ref, gpost_ref, o_ref, h_ref, *, n_chunk):
    j = pl.program_id(1)

    tm = x_ref.shape[0]

    @pl.when(j == 0)
    def _():
        for r in range(0, tm, NORM_ROWS):
            h_ref[r:r + NORM_ROWS, :] = _rms(x_ref[r:r + NORM_ROWS, :], gpre_ref[...]).astype(BF16)
        o_ref[...] = jnp.zeros(o_ref.shape, F32)

    h = h_ref[...]
    a = jnp.dot(h, wa_ref[...].astype(BF16), preferred_element_type=F32)
    u = jnp.dot(h, wu_ref[...].astype(BF16), preferred_element_type=F32)
    act = (a * jax.nn.sigmoid(a) * u).astype(BF16)
    for n in range(0, D_MODEL, n_chunk):
        o_ref[:, n:n + n_chunk] += jnp.dot(act, wo_ref[:, n:n + n_chunk].astype(BF16),
                                           preferred_element_type=F32)

    @pl.when(j == pl.num_programs(1) - 1)
    def _():
        g_half = 0.5 * gpost_ref[...]
        for r in range(0, tm, NORM_ROWS):
            rows = slice(r, r + NORM_ROWS)
            o_ref[rows, :] = x_ref[rows, :] + _rms(o_ref[rows, :], g_half)


def _ffn(x, g_pre, w_in, w_out, g_post, layer, tm=1024, tf=256, n_chunk=512):
    t = x.shape[0]
    nf = D_FF // tf
    return pl.pallas_call(
        functools.partial(_ffn_kernel, n_chunk=n_chunk),
        grid=(t // tm, nf),
        in_specs=[
            pl.BlockSpec((tm, D_MODEL), lambda i, j: (i, 0)),
            _layer_vec(D_MODEL, layer),
            pl.BlockSpec((None, D_MODEL, tf), lambda i, j: (layer, 0, j)),
            pl.BlockSpec((None, D_MODEL, tf), lambda i, j: (layer, 0, nf + j)),
            pl.BlockSpec((None, tf, D_MODEL), lambda i, j: (layer, j, 0)),
            _layer_vec(D_MODEL, layer),
        ],
        out_specs=pl.BlockSpec((tm, D_MODEL), lambda i, j: (i, 0)),
        out_shape=jax.ShapeDtypeStruct((t, D_MODEL), F32),
        scratch_shapes=[pltpu.VMEM((tm, D_MODEL), BF16)],
        compiler_params=_params("parallel", "arbitrary", vmem=FFN_VMEM_LIMIT),
        name="ffn",
    )(x, g_pre, w_in, w_in, w_out, g_post)


def _proj_kernel(x_ref, gpre_ref, win_ref, gcq_ref, wuq_ref, gckv_ref, wukv_ref, cos_ref, sin_ref,
                 qn_ref, qr_ref, kn_ref, kr_ref, v_ref, g_ref, sq_ref, sk_ref, sv_ref):
    h = _rms(x_ref[...], gpre_ref[...]).astype(BF16)
    u = jnp.dot(h, win_ref[...], preferred_element_type=F32)
    cos = cos_ref[...]
    sin = sin_ref[...]
    lane = lax.broadcasted_iota(jnp.int32, cos.shape, 1)
    upper = lane >= MLA_ROPE

    cq = _rms(u[:, _U_CQ:_U_CKV], gcq_ref[...]).astype(BF16)
    q = jnp.dot(cq, wuq_ref[...], preferred_element_type=F32) * MLA_SCORE_SCALE
    n_nope = MLA_HEADS * MLA_NOPE
    n_rope = MLA_HEADS * MLA_ROPE
    qn_ref[...] = q[:, :n_nope].astype(BF16)
    for pair in range(MLA_HEADS // 2):
        lo = n_nope + pair * LANES
        roped = q[:, lo:lo + LANES] * cos + q[:, lo + n_rope:lo + n_rope + LANES] * sin
        qr_ref[:, (2 * pair) * LANES:(2 * pair + 1) * LANES] = jnp.where(upper, 0.0, roped).astype(BF16)
        qr_ref[:, (2 * pair + 1) * LANES:(2 * pair + 2) * LANES] = jnp.where(upper, roped, 0.0).astype(BF16)

    ckv = _rms(u[:, _U_CKV:_U_KR], gckv_ref[...]).astype(BF16)
    kv = jnp.dot(ckv, wukv_ref[...], preferred_element_type=F32)
    kn_ref[...] = kv[:, :n_nope].astype(BF16)
    v_ref[...] = kv[:, n_nope:].astype(BF16)
    t = u[:, _U_KR:_U_CONV] * jnp.where(upper, sin, cos)
    kr_ref[...] = (t + pltpu.roll(t, MLA_ROPE, 1)).astype(BF16)

    a = u[:, _U_CONV:_U_CONV + CONV_CH]
    gate = u[:, _U_CONV + CONV_CH:_U_SB]
    g_ref[...] = a * jax.nn.sigmoid(gate)

    n_sb = SB_HEADS * SB_HEAD_DIM
    sq_ref[...] = (u[:, _U_SB:_U_SB + n_sb] * SB_HEAD_DIM ** -0.5).astype(BF16)
    sk_ref[...] = u[:, _U_SB + n_sb:_U_SB + 2 * n_sb].astype(BF16)
    sv_ref[...] = u[:, _U_SB + 2 * n_sb:_U_END].astype(BF16)


def _proj(x, g_pre, w_in_ext, g_cq, w_uq_ext, g_ckv, w_ukv_ext, cos_t, sin_t, layer, tm=256):
    t = x.shape[0]
    n_nope = MLA_HEADS * MLA_NOPE
    n_sb = SB_HEADS * SB_HEAD_DIM
    row = lambda n: pl.BlockSpec((tm, n), lambda i: (i, 0))
    out_cols = [(n_nope, BF16), (n_nope, BF16), (n_nope, BF16), (LANES, BF16), (n_nope, BF16),
                (CONV_CH, F32), (n_sb, BF16), (n_sb, BF16), (n_sb, BF16)]
    return pl.pallas_call(
        _proj_kernel,
        grid=(t // tm,),
        in_specs=[
            row(D_MODEL), _layer_vec(D_MODEL, layer), _layer_mat(w_in_ext.shape[1:], layer),
            _layer_vec(Q_LORA, layer), _layer_mat(w_uq_ext.shape[1:], layer),
            _layer_vec(KV_LORA, layer), _layer_mat(w_ukv_ext.shape[1:], layer),
            row(LANES), row(LANES),
        ],
        out_specs=[row(n) for n, _ in out_cols],
        out_shape=[jax.ShapeDtypeStruct((t, n), dt) for n, dt in out_cols],
        compiler_params=_params("parallel"),
        name="mixer_proj",
    )(x, g_pre, w_in_ext, g_cq, w_uq_ext, g_ckv, w_ukv_ext, cos_t, sin_t)


def _head(h):
    return slice(h * LANES, (h + 1) * LANES)


def _mla_kernel(qn_ref, qr_ref, kn_ref, kr_ref, v_ref, o_ref, m_ref, l_ref, acc_ref, *, tile, heads):
    qi = pl.program_id(2)
    m_ref[...] = jnp.full(m_ref.shape, NEG, F32)
    l_ref[...] = jnp.zeros(l_ref.shape, F32)
    acc_ref[...] = jnp.zeros(acc_ref.shape, F32)

    def step(kj, diagonal):
        ks = pl.multiple_of(kj * tile, tile)
        kr = kr_ref[pl.ds(ks, tile), :]
        for h in range(heads):
            q = jnp.concatenate([qn_ref[:, _head(h)], qr_ref[:, _head(h)]], axis=-1)
            k = jnp.concatenate([kn_ref[pl.ds(ks, tile), _head(h)], kr], axis=-1)
            s = lax.dot_general(q, k, (((1,), (1,)), ((), ())), preferred_element_type=F32)
            if diagonal:
                row = lax.broadcasted_iota(jnp.int32, s.shape, 0)
                col = lax.broadcasted_iota(jnp.int32, s.shape, 1)
                s = jnp.where(col <= row, s, NEG)
            m_prev = m_ref[h]
            m_new = jnp.maximum(m_prev, jnp.max(s, axis=-1, keepdims=True))
            alpha = jnp.exp2(m_prev - m_new)
            p = jnp.exp2(s - jnp.concatenate([m_new] * (tile // LANES), axis=-1))
            l_ref[h] = alpha * l_ref[h] + jnp.sum(p, axis=-1, keepdims=True)
            pv = jnp.dot(p.astype(BF16), v_ref[pl.ds(ks, tile), _head(h)], preferred_element_type=F32)
            acc_ref[h] = alpha * acc_ref[h] + pv
            m_ref[h] = m_new

    def body(kj, carry):
        step(kj, False)
        return carry

    lax.fori_loop(0, qi, body, 0)
    step(qi, True)
    for h in range(heads):
        o_ref[:, _head(h)] = (acc_ref[h] / l_ref[h]).astype(BF16)


def _mla_attention(qn, qr, kn, kr, v, batch, seq, tile=512, heads=4):
    t = qn.shape[0]
    nq = seq // tile
    qspec = pl.BlockSpec((tile, heads * LANES), lambda b, h, i: (b * nq + i, h))
    kspec = pl.BlockSpec((seq, heads * LANES), lambda b, h, i: (b, h))
    return pl.pallas_call(
        functools.partial(_mla_kernel, tile=tile, heads=heads),
        grid=(batch, MLA_HEADS // heads, nq),
        in_specs=[qspec, qspec, kspec, pl.BlockSpec((seq, LANES), lambda b, h, i: (b, 0)), kspec],
        out_specs=qspec,
        out_shape=jax.ShapeDtypeStruct((t, MLA_HEADS * MLA_V), BF16),
        scratch_shapes=[pltpu.VMEM((heads, tile, LANES), F32)] * 3,
        compiler_params=_params("parallel", "parallel", "arbitrary"),
        name="mla_attention",
    )(qn, qr, kn, kr, v)


def _sb_kernel(q_ref, k_ref, v_ref, o_ref, run_ref, acc_ref, *, tile, heads):
    qi = pl.program_id(2)
    blk = MXU_DIM
    uj = lax.broadcasted_iota(jnp.int32, (blk, blk), 0)
    us = lax.broadcasted_iota(jnp.int32, (blk, blk), 1)
    suffix = jnp.where(uj > us, 1.0, 0.0).astype(BF16)
    run_ref[...] = jnp.zeros(run_ref.shape, F32)
    acc_ref[...] = jnp.zeros(acc_ref.shape, F32)

    def step(kj, diagonal):
        ks = pl.multiple_of(kj * tile, tile)
        for h in range(heads):
            z = lax.dot_general(q_ref[:, _head(h)], k_ref[pl.ds(ks, tile), _head(h)],
                                (((1,), (1,)), ((), ())), preferred_element_type=F32)
            softplus = jnp.log(1.0 + jnp.exp2(jnp.abs(z) * -LOG2E))
            log_beta = jnp.minimum(z, 0.0) - softplus
            log_keep = log_beta - z
            if diagonal:
                row = lax.broadcasted_iota(jnp.int32, z.shape, 0)
                col = lax.broadcasted_iota(jnp.int32, z.shape, 1)
                strict = col < row
                log_keep = jnp.where(strict, log_keep, 0.0)
            run = run_ref[h]
            after = [None] * (tile // blk)
            for c in reversed(range(tile // blk)):
                part = log_keep[:, c * blk:(c + 1) * blk]
                sums = jnp.dot(part.astype(BF16), suffix, preferred_element_type=F32)
                after[c] = sums + jnp.concatenate([run] * (blk // LANES), axis=-1)
                run = run + (sums[:, 0:1] + part[:, 0:1])
            w = jnp.exp(log_beta + jnp.concatenate(after, axis=-1))
            if diagonal:
                w = jnp.where(strict, w, 0.0)
            acc_ref[h] += jnp.dot(w.astype(BF16), v_ref[pl.ds(ks, tile), _head(h)],
                                  preferred_element_type=F32)
            run_ref[h] = run

    step(qi, True)

    def body(i, carry):
        step(qi - 1 - i, False)
        return carry

    lax.fori_loop(0, qi, body, 0)
    for h in range(heads):
        o_ref[:, _head(h)] = acc_ref[h].astype(BF16)


def _sb_attention(q, k, v, batch, seq, tile=512, heads=4):
    t = q.shape[0]
    nq = seq // tile
    qspec = pl.BlockSpec((tile, heads * LANES), lambda b, h, i: (b * nq + i, h))
    kspec = pl.BlockSpec((seq, heads * LANES), lambda b, h, i: (b, h))
    stat = pltpu.VMEM((heads, tile, LANES), F32)
    return pl.pallas_call(
        functools.partial(_sb_kernel, tile=tile, heads=heads),
        grid=(batch, SB_HEADS // heads, nq),
        in_specs=[qspec, kspec, kspec],
        out_specs=qspec,
        out_shape=jax.ShapeDtypeStruct((t, SB_HEADS * SB_HEAD_DIM), BF16),
        scratch_shapes=[stat, stat],
        compiler_params=_params("parallel", "parallel", "arbitrary"),
        name="sb_attention",
    )(q, k, v)


def _mix_out_kernel(x_ref, omla_ref, g_ref, gprev_ref, osb_ref, wdw_ref, bdw_ref, gln_ref, bln_ref,
                    wpw_ref, wout_ref, gpost_ref, o_ref, gext_ref, shift_ref, *, tm, tiles_per_seq):
    i = pl.program_id(0)
    n_mla = MLA_HEADS * MLA_V
    first = (i % tiles_per_seq) == 0
    gext_ref[0:CONV_HALO, :] = jnp.where(first, 0.0, gprev_ref[...])
    gext_ref[CONV_HALO:, :] = g_ref[...]
    base = CONV_HALO - (CONV_WIDTH - 1)
    phase_offs = [[o for o in range(base, base + CONV_WIDTH) if o % SUBLANES == ph] for ph in range(SUBLANES)]
    mix = (jnp.dot(omla_ref[...], wout_ref[:n_mla, :], preferred_element_type=F32)
           + jnp.dot(osb_ref[...], wout_ref[n_mla + CONV_CH:, :], preferred_element_type=F32))
    y = jnp.zeros((tm, CONV_CH), F32) + bdw_ref[...]
    for phase, offs in enumerate(phase_offs):
        rows = tm + offs[-1] - phase
        shift_ref[phase, 0:rows, :] = gext_ref[pl.ds(phase, rows), :]
        for o in offs:
            y = y + shift_ref[phase, o - phase:o - phase + tm, :] * wdw_ref[o - base:o - base + 1, :]
    mu = jnp.mean(y, axis=-1, keepdims=True)
    d = y - mu
    var = jnp.mean(d * d, axis=-1, keepdims=True)
    yn = d * lax.rsqrt(var + EPS) * gln_ref[...] + bln_ref[...]
    act = (yn * jax.nn.sigmoid(yn)).astype(BF16)
    oconv = jnp.dot(act, wpw_ref[...], preferred_element_type=F32).astype(BF16)
    mix = mix + jnp.dot(oconv, wout_ref[n_mla:n_mla + CONV_CH, :], preferred_element_type=F32)
    o_ref[...] = x_ref[...] + _rms(mix, gpost_ref[...])


def _mix_out(x, o_mla, g, o_sb, w_dw, b_dw, g_ln, b_ln, w_pw, w_out, g_post, seq, layer, tm=512):
    t = x.shape[0]
    tiles_per_seq = seq // tm
    halo_blocks = tm // CONV_HALO
    row = lambda n: pl.BlockSpec((tm, n), lambda i: (i, 0))
    prev = pl.BlockSpec((CONV_HALO, CONV_CH), lambda i: (jnp.maximum(i * halo_blocks - 1, 0), 0))
    return pl.pallas_call(
        functools.partial(_mix_out_kernel, tm=tm, tiles_per_seq=tiles_per_seq),
        grid=(t // tm,),
        in_specs=[
            row(D_MODEL), row(MLA_HEADS * MLA_V), row(CONV_CH), prev, row(SB_HEADS * SB_HEAD_DIM),
            _layer_mat((CONV_WIDTH, CONV_CH), layer), _layer_vec(CONV_CH, layer), _layer_vec(CONV_CH, layer),
            _layer_vec(CONV_CH, layer), _layer_mat((CONV_CH, CONV_CH), layer),
            _layer_mat((D_MIX, D_MODEL), layer), _layer_vec(D_MODEL, layer),
        ],
        out_specs=row(D_MODEL),
        out_shape=jax.ShapeDtypeStruct((t, D_MODEL), F32),
        scratch_shapes=[pltpu.VMEM((tm + CONV_HALO, CONV_CH), F32),
                        pltpu.VMEM((SUBLANES, tm + CONV_HALO, CONV_CH), F32)],
        compiler_params=_params("parallel"),
        name="mix_out",
    )(x, o_mla, g, g, o_sb, w_dw, b_dw, g_ln, b_ln, w_pw, w_out, g_post)


def _ple_kernel(x_ref, p_ref, gpre_ref, wg_ref, wp_ref, gpost_ref, o_ref):
    x = x_ref[...]
    h = _rms(x, gpre_ref[...]).astype(BF16)
    gate = jax.nn.sigmoid(jnp.dot(h, wg_ref[...], preferred_element_type=F32))
    e = jnp.dot(p_ref[...].astype(BF16), wp_ref[...], preferred_element_type=F32)
    o_ref[...] = x + _rms(gate * e, gpost_ref[...])


def _ple(x, p, g_pre, w_gate, w_proj, g_post, layer, tm=512):
    t = x.shape[0]
    row = lambda n: pl.BlockSpec((tm, n), lambda i: (i, 0))
    return pl.pallas_call(
        _ple_kernel,
        grid=(t // tm,),
        in_specs=[row(D_MODEL), pl.BlockSpec((None, tm, D_PLE), lambda i: (layer, i, 0)),
                  _layer_vec(D_MODEL, layer), _layer_mat((D_MODEL, D_MODEL), layer),
                  _layer_mat((D_PLE, D_MODEL), layer), _layer_vec(D_MODEL, layer)],
        out_specs=row(D_MODEL),
        out_shape=jax.ShapeDtypeStruct((t, D_MODEL), F32),
        compiler_params=_params("parallel"),
        name="ple",
    )(x, p, g_pre, w_gate, w_proj, g_post)


def _prep_w_in(w_in):
    half = MLA_ROPE // 2
    kr = w_in[:, :, _U_KR:_U_KR + MLA_ROPE]
    kr_sw = jnp.concatenate([kr[:, :, half:], kr[:, :, :half]], axis=2)
    return jnp.concatenate([w_in[:, :, :_U_KR + MLA_ROPE], kr_sw, w_in[:, :, _U_KR + MLA_ROPE:]],
                           axis=2).astype(BF16)


def _prep_w_uq(w_uq):
    depth = w_uq.shape[0]
    half = MLA_ROPE // 2
    w = w_uq.reshape(depth, Q_LORA, MLA_HEADS, MLA_QK)
    nope = w[..., :MLA_NOPE].reshape(depth, Q_LORA, -1)
    rope = w[..., MLA_NOPE:]
    rope_sw = jnp.concatenate([rope[..., half:], rope[..., :half]], axis=-1)
    return jnp.concatenate([nope, rope.reshape(depth, Q_LORA, -1), rope_sw.reshape(depth, Q_LORA, -1)],
                           axis=2).astype(BF16)


def _prep_w_ukv(w_ukv):
    depth = w_ukv.shape[0]
    w = w_ukv.reshape(depth, KV_LORA, MLA_HEADS, MLA_NOPE + MLA_V)
    return jnp.concatenate([w[..., :MLA_NOPE].reshape(depth, KV_LORA, -1),
                            w[..., MLA_NOPE:].reshape(depth, KV_LORA, -1)], axis=2).astype(BF16)


def kernel(x, p, positions, g_ff1_pre, w_ff1_in, w_ff1_out, g_ff1_post, g_mix_pre, w_in, g_cq, w_uq, g_ckv, w_ukv, w_dw, b_dw, g_conv_ln, b_conv_ln, w_pw, w_out, g_mix_post, g_ff2_pre, w_ff2_in, w_ff2_out, g_ff2_post, g_ple_pre, w_ple_gate, w_ple_proj, g_ple_post):
    batch, seq, _ = x.shape
    depth = p.shape[0]
    t = batch * seq
    xt = x.reshape(t, D_MODEL)
    pt = p.reshape(depth, t, D_PLE)
    cos_t, sin_t = _rope_tables(positions)
    vec = lambda a: a[:, None, :]
    w_in_ext, w_uq_ext, w_ukv_ext = _prep_w_in(w_in), _prep_w_uq(w_uq), _prep_w_ukv(w_ukv)
    w_pw_b, w_out_b = w_pw.astype(BF16), w_out.astype(BF16)
    w_gate_b, w_proj_b = w_ple_gate.astype(BF16), w_ple_proj.astype(BF16)
    for i in range(depth):
        xt = _ffn(xt, vec(g_ff1_pre), w_ff1_in, w_ff1_out, vec(g_ff1_post), i)
        qn, qr, kn, kr, v, g, sq, sk, sv = _proj(
            xt, vec(g_mix_pre), w_in_ext, vec(g_cq), w_uq_ext, vec(g_ckv), w_ukv_ext, cos_t, sin_t, i)
        o_mla = _mla_attention(qn, qr, kn, kr, v, batch, seq)
        o_sb = _sb_attention(sq, sk, sv, batch, seq)
        xt = _mix_out(xt, o_mla, g, o_sb, w_dw, vec(b_dw), vec(g_conv_ln), vec(b_conv_ln),
                      w_pw_b, w_out_b, vec(g_mix_post), seq, i)
        xt = _ffn(xt, vec(g_ff2_pre), w_ff2_in, w_ff2_out, vec(g_ff2_post), i)
        xt = _ple(xt, pt, vec(g_ple_pre), w_gate_b, w_proj_b, vec(g_ple_post), i)
    return xt.reshape(batch, seq, D_MODEL)
```

```python
import functools

import jax
import jax.numpy as jnp
from jax import lax
from jax.experimental import pallas as pl
from jax.experimental.pallas import tpu as pltpu

F32 = jnp.float32
BF16 = jnp.bfloat16

D_MODEL = 2048
D_PLE = 256
D_FF = 5632
MLA_HEADS = 8
MLA_NOPE = 128
MLA_ROPE = 64
MLA_QK = MLA_NOPE + MLA_ROPE
MLA_V = 128
Q_LORA = 512
KV_LORA = 256
ROPE_THETA = 10000.0
CONV_CH = 512
CONV_WIDTH = 31
SB_HEADS = 4
SB_HEAD_DIM = 128
D_MIX = MLA_HEADS * MLA_V + CONV_CH + SB_HEADS * SB_HEAD_DIM
EPS = 1e-6
NEG = -1e30

LOG2E = 1.4426950408889634
LN2 = 0.6931471805599453
MLA_SCORE_SCALE = MLA_QK ** -0.5 * LOG2E

LANES = 128
SUBLANES = 8
MXU_DIM = 256
PRE_NORM_ROWS = 128
LAST_ROWS = 256
CONV_HALO = 32

_U_CQ = 0
_U_CKV = Q_LORA
_U_KR = Q_LORA + KV_LORA
_U_CONV = _U_KR + 2 * MLA_ROPE
_U_SB = _U_CONV + 2 * CONV_CH
_U_END = _U_SB + 3 * SB_HEADS * SB_HEAD_DIM

VMEM_LIMIT = 48 * 1024 * 1024
FFN_VMEM_LIMIT = 58 * 1024 * 1024


def _params(*sem, vmem=VMEM_LIMIT):
    return pltpu.CompilerParams(dimension_semantics=sem, vmem_limit_bytes=vmem)


def _resident(shape):
    nd = len(shape)
    return pl.BlockSpec(shape, lambda *_: (0,) * nd, pipeline_mode=pl.Buffered(1))


def _layer_vec(n, layer):
    return pl.BlockSpec((None, 1, n), lambda *_: (layer, 0, 0), pipeline_mode=pl.Buffered(1))


def _layer_mat(shape, layer):
    return pl.BlockSpec((None,) + tuple(shape), lambda *_: (layer, 0, 0), pipeline_mode=pl.Buffered(1))


def _rms(x, g):
    return x * lax.rsqrt(jnp.mean(x * x, axis=-1, keepdims=True) + EPS) * g


def _rope_table_kernel(pos_ref, freq_ref, cos_ref, sin_ref):
    ang = pos_ref[...].astype(F32) * freq_ref[...]
    lane = lax.broadcasted_iota(jnp.int32, ang.shape, 1)
    first_half = (lane & (MLA_ROPE - 1)) < (MLA_ROPE // 2)
    cos_ref[...] = jnp.cos(ang)
    sin_ref[...] = jnp.where(first_half, -jnp.sin(ang), jnp.sin(ang))


def _rope_tables(positions, tm=1024):
    t = positions.size
    half = MLA_ROPE // 2
    inv_freq = ROPE_THETA ** (-jnp.arange(0, MLA_ROPE, 2, dtype=F32) / MLA_ROPE)
    freq = jnp.tile(inv_freq, LANES // half)[None, :]
    pos = positions.reshape(t, 1)
    return pl.pallas_call(
        _rope_table_kernel,
        grid=(t // tm,),
        in_specs=[pl.BlockSpec((tm, 1), lambda i: (i, 0)), _resident((1, LANES))],
        out_specs=[pl.BlockSpec((tm, LANES), lambda i: (i, 0))] * 2,
        out_shape=[jax.ShapeDtypeStruct((t, LANES), F32)] * 2,
        compiler_params=_params("parallel"),
        name="rope_tables",
    )(pos, freq)


def _ffn_kernel(x_ref, gpre_ref, wa_ref, wu_ref, wo_ref, gpost_ref, o_ref, h_ref, *, n_chunk):
    j = pl.program_id(1)

    tm = x_ref.shape[0]

    last = pl.num_programs(1) - 1

    def activation():
        h = h_ref[...]
        a = jnp.dot(h, wa_ref[...].astype(BF16), preferred_element_type=F32)
        u = jnp.dot(h, wu_ref[...].astype(BF16), preferred_element_type=F32)
        return (a * jax.nn.sigmoid(a) * u).astype(BF16)

    def accumulate(first):
        act = activation()
        for n in range(0, D_MODEL, n_chunk):
            part = jnp.dot(act, wo_ref[:, n:n + n_chunk].astype(BF16), preferred_element_type=F32)
            if first:
                o_ref[:, n:n + n_chunk] = part
            else:
                o_ref[:, n:n + n_chunk] += part

    @pl.when(j == 0)
    def _():
        for r in range(0, tm, PRE_NORM_ROWS):
            h_ref[r:r + PRE_NORM_ROWS, :] = _rms(x_ref[r:r + PRE_NORM_ROWS, :], gpre_ref[...]).astype(BF16)
        accumulate(True)

    @pl.when((j > 0) & (j < last))
    def _():
        accumulate(False)

    @pl.when(j == last)
    def _():
        act = activation()
        wo = wo_ref[...].astype(BF16)
        g_half = 0.5 * gpost_ref[...]
        for r in range(0, tm, LAST_ROWS):
            rows = slice(r, r + LAST_ROWS)
            acc = o_ref[rows, :] + jnp.dot(act[rows, :], wo, preferred_element_type=F32)
            o_ref[rows, :] = x_ref[rows, :] + _rms(acc, g_half)


def _ffn(x, g_pre, w_in, w_out, g_post, layer, tm=1024, tf=256, n_chunk=512):
    t = x.shape[0]
    nf = D_FF // tf
    return pl.pallas_call(
        functools.partial(_ffn_kernel, n_chunk=n_chunk),
        grid=(t // tm, nf),
        in_specs=[
            pl.BlockSpec((tm, D_MODEL), lambda i, j: (i, 0)),
            _layer_vec(D_MODEL, layer),
            pl.BlockSpec((None, D_MODEL, tf), lambda i, j: (layer, 0, j)),
            pl.BlockSpec((None, D_MODEL, tf), lambda i, j: (layer, 0, nf + j)),
            pl.BlockSpec((None, tf, D_MODEL), lambda i, j: (layer, j, 0)),
            _layer_vec(D_MODEL, layer),
        ],
        out_specs=pl.BlockSpec((tm, D_MODEL), lambda i, j: (i, 0)),
        out_shape=jax.ShapeDtypeStruct((t, D_MODEL), F32),
        scratch_shapes=[pltpu.VMEM((tm, D_MODEL), BF16)],
        compiler_params=_params("parallel", "arbitrary", vmem=FFN_VMEM_LIMIT),
        name="ffn",
    )(x, g_pre, w_in, w_in, w_out, g_post)


def _proj_kernel(x_ref, gpre_ref, win_ref, gcq_ref, wuq_ref, gckv_ref, wukv_ref, cos_ref, sin_ref,
                 qn_ref, qr_ref, kn_ref, kr_ref, v_ref, g_ref, sq_ref, sk_ref, sv_ref):
    h = _rms(x_ref[...], gpre_ref[...]).astype(BF16)
    u = jnp.dot(h, win_ref[...], preferred_element_type=F32)
    cos = cos_ref[...]
    sin = sin_ref[...]
    lane = lax.broadcasted_iota(jnp.int32, cos.shape, 1)
    upper = lane >= MLA_ROPE

    cq = _rms(u[:, _U_CQ:_U_CKV], gcq_ref[...]).astype(BF16)
    q = jnp.dot(cq, wuq_ref[...], preferred_element_type=F32) * MLA_SCORE_SCALE
    n_nope = MLA_HEADS * MLA_NOPE
    n_rope = MLA_HEADS * MLA_ROPE
    qn_ref[...] = q[:, :n_nope].astype(BF16)
    for pair in range(MLA_HEADS // 2):
        lo = n_nope + pair * LANES
        roped = q[:, lo:lo + LANES] * cos + q[:, lo + n_rope:lo + n_rope + LANES] * sin
        qr_ref[:, (2 * pair) * LANES:(2 * pair + 1) * LANES] = jnp.where(upper, 0.0, roped).astype(BF16)
        qr_ref[:, (2 * pair + 1) * LANES:(2 * pair + 2) * LANES] = jnp.where(upper, roped, 0.0).astype(BF16)

    ckv = _rms(u[:, _U_CKV:_U_KR], gckv_ref[...]).astype(BF16)
    kv = jnp.dot(ckv, wukv_ref[...], preferred_element_type=F32)
    kn_ref[...] = kv[:, :n_nope].astype(BF16)
    v_ref[...] = kv[:, n_nope:].astype(BF16)
    t = u[:, _U_KR:_U_CONV] * jnp.where(upper, sin, cos)
    kr_ref[...] = (t + pltpu.roll(t, MLA_ROPE, 1)).astype(BF16)

    a = u[:, _U_CONV:_U_CONV + CONV_CH]
    gate = u[:, _U_CONV + CONV_CH:_U_SB]
    g_ref[...] = a * jax.nn.sigmoid(gate)

    n_sb = SB_HEADS * SB_HEAD_DIM
    sq_ref[...] = (u[:, _U_SB:_U_SB + n_sb] * SB_HEAD_DIM ** -0.5).astype(BF16)
    sk_ref[...] = u[:, _U_SB + n_sb:_U_SB + 2 * n_sb].astype(BF16)
    sv_ref[...] = u[:, _U_SB + 2 * n_sb:_U_END].astype(BF16)


def _proj(x, g_pre, w_in_ext, g_cq, w_uq_ext, g_ckv, w_ukv_ext, cos_t, sin_t, layer, tm=256):
    t = x.shape[0]
    n_nope = MLA_HEADS * MLA_NOPE
    n_sb = SB_HEADS * SB_HEAD_DIM
    row = lambda n: pl.BlockSpec((tm, n), lambda i: (i, 0))
    out_cols = [(n_nope, BF16), (n_nope, BF16), (n_nope, BF16), (LANES, BF16), (n_nope, BF16),
                (CONV_CH, F32), (n_sb, BF16), (n_sb, BF16), (n_sb, BF16)]
    return pl.pallas_call(
        _proj_kernel,
        grid=(t // tm,),
        in_specs=[
            row(D_MODEL), _layer_vec(D_MODEL, layer), _layer_mat(w_in_ext.shape[1:], layer),
            _layer_vec(Q_LORA, layer), _layer_mat(w_uq_ext.shape[1:], layer),
            _layer_vec(KV_LORA, layer), _layer_mat(w_ukv_ext.shape[1:], layer),
            row(LANES), row(LANES),
        ],
        out_specs=[row(n) for n, _ in out_cols],
        out_shape=[jax.ShapeDtypeStruct((t, n), dt) for n, dt in out_cols],
        compiler_params=_params("parallel"),
        name="mixer_proj",
    )(x, g_pre, w_in_ext, g_cq, w_uq_ext, g_ckv, w_ukv_ext, cos_t, sin_t)


def _head(h):
    return slice(h * LANES, (h + 1) * LANES)


def _mla_kernel(qn_ref, qr_ref, kn_ref, kr_ref, v_ref, o_ref, m_ref, l_ref, acc_ref, *, tile, heads):
    qi = pl.program_id(2)
    m_ref[...] = jnp.full(m_ref.shape, NEG, F32)
    l_ref[...] = jnp.zeros(l_ref.shape, F32)
    acc_ref[...] = jnp.zeros(acc_ref.shape, F32)

    def step(kj, diagonal):
        ks = pl.multiple_of(kj * tile, tile)
        kr = kr_ref[pl.ds(ks, tile), :]
        for h in range(heads):
            q = jnp.concatenate([qn_ref[:, _head(h)], qr_ref[:, _head(h)]], axis=-1)
            k = jnp.concatenate([kn_ref[pl.ds(ks, tile), _head(h)], kr], axis=-1)
            s = lax.dot_general(q, k, (((1,), (1,)), ((), ())), preferred_element_type=F32)
            if diagonal:
                row = lax.broadcasted_iota(jnp.int32, s.shape, 0)
                col = lax.broadcasted_iota(jnp.int32, s.shape, 1)
                s = jnp.where(col <= row, s, NEG)
            m_prev = m_ref[h]
            m_new = jnp.maximum(m_prev, jnp.max(s, axis=-1, keepdims=True))
            alpha = jnp.exp2(m_prev - m_new)
            p = jnp.exp2(s - jnp.concatenate([m_new] * (tile // LANES), axis=-1))
            l_ref[h] = alpha * l_ref[h] + jnp.sum(p, axis=-1, keepdims=True)
            pv = jnp.dot(p.astype(BF16), v_ref[pl.ds(ks, tile), _head(h)], preferred_element_type=F32)
            acc_ref[h] = alpha * acc_ref[h] + pv
            m_ref[h] = m_new

    def body(kj, carry):
        step(kj, False)
        return carry

    lax.fori_loop(0, qi, body, 0)
    step(qi, True)
    for h in range(heads):
        o_ref[:, _head(h)] = (acc_ref[h] / l_ref[h]).astype(BF16)


def _mla_attention(qn, qr, kn, kr, v, batch, seq, tile=512, heads=4):
    t = qn.shape[0]
    nq = seq // tile
    qspec = pl.BlockSpec((tile, heads * LANES), lambda b, h, i: (b * nq + i, h))
    kspec = pl.BlockSpec((seq, heads * LANES), lambda b, h, i: (b, h))
    return pl.pallas_call(
        functools.partial(_mla_kernel, tile=tile, heads=heads),
        grid=(batch, MLA_HEADS // heads, nq),
        in_specs=[qspec, qspec, kspec, pl.BlockSpec((seq, LANES), lambda b, h, i: (b, 0)), kspec],
        out_specs=qspec,
        out_shape=jax.ShapeDtypeStruct((t, MLA_HEADS * MLA_V), BF16),
        scratch_shapes=[pltpu.VMEM((heads, tile, LANES), F32)] * 3,
        compiler_params=_params("parallel", "parallel", "arbitrary"),
        name="mla_attention",
    )(qn, qr, kn, kr, v)


def _sb_kernel(q_ref, k_ref, v_ref, o_ref, run_ref, acc_ref, *, tile, heads):
    qi = pl.program_id(2)
    blk = MXU_DIM
    uj = lax.broadcasted_iota(jnp.int32, (blk, blk), 0)
    us = lax.broadcasted_iota(jnp.int32, (blk, blk), 1)
    suffix = jnp.where(uj > us, 1.0, 0.0).astype(BF16)
    run_ref[...] = jnp.zeros(run_ref.shape, F32)
    acc_ref[...] = jnp.zeros(acc_ref.shape, F32)

    def step(kj, diagonal):
        ks = pl.multiple_of(kj * tile, tile)
        for h in range(heads):
            z = lax.dot_general(q_ref[:, _head(h)], k_ref[pl.ds(ks, tile), _head(h)],
                                (((1,), (1,)), ((), ())), preferred_element_type=F32)
            softplus = jnp.log(1.0 + jnp.exp2(jnp.abs(z) * -LOG2E))
            log_beta = jnp.minimum(z, 0.0) - softplus
            log_keep = log_beta - z
            if diagonal:
                row = lax.broadcasted_iota(jnp.int32, z.shape, 0)
                col = lax.broadcasted_iota(jnp.int32, z.shape, 1)
                strict = col < row
                log_keep = jnp.where(strict, log_keep, 0.0)
            run = run_ref[h]
            after = [None] * (tile // blk)
            for c in reversed(range(tile // blk)):
                part = log_keep[:, c * blk:(c + 1) * blk]
                sums = jnp.dot(part.astype(BF16), suffix, preferred_element_type=F32)
                after[c] = sums + jnp.concatenate([run] * (blk // LANES), axis=-1)
                run = run + (sums[:, 0:1] + part[:, 0:1])
            w = jnp.exp(log_beta + jnp.concatenate(after, axis=-1))
            if diagonal:
                w = jnp.where(strict, w, 0.0)
            acc_ref[h] += jnp.dot(w.astype(BF16), v_ref[pl.ds(ks, tile), _head(h)],
                                  preferred_element_type=F32)
            run_ref[h] = run

    step(qi, True)

    def body(i, carry):
        step(qi - 1 - i, False)
        return carry

    lax.fori_loop(0, qi, body, 0)
    for h in range(heads):
        o_ref[:, _head(h)] = acc_ref[h].astype(BF16)


def _sb_attention(q, k, v, batch, seq, tile=512, heads=4):
    t = q.shape[0]
    nq = seq // tile
    qspec = pl.BlockSpec((tile, heads * LANES), lambda b, h, i: (b * nq + i, h))
    kspec = pl.BlockSpec((seq, heads * LANES), lambda b, h, i: (b, h))
    stat = pltpu.VMEM((heads, tile, LANES), F32)
    return pl.pallas_call(
        functools.partial(_sb_kernel, tile=tile, heads=heads),
        grid=(batch, SB_HEADS // heads, nq),
        in_specs=[qspec, kspec, kspec],
        out_specs=qspec,
        out_shape=jax.ShapeDtypeStruct((t, SB_HEADS * SB_HEAD_DIM), BF16),
        scratch_shapes=[stat, stat],
        compiler_params=_params("parallel", "parallel", "arbitrary"),
        name="sb_attention",
    )(q, k, v)


def _mix_out_kernel(x_ref, omla_ref, g_ref, gprev_ref, osb_ref, wdw_ref, bdw_ref, gln_ref, bln_ref,
                    wpw_ref, wout_ref, gpost_ref, o_ref, gext_ref, shift_ref, *, tm, tiles_per_seq):
    i = pl.program_id(0)
    n_mla = MLA_HEADS * MLA_V
    first = (i % tiles_per_seq) == 0
    gext_ref[0:CONV_HALO, :] = jnp.where(first, 0.0, gprev_ref[...])
    gext_ref[CONV_HALO:, :] = g_ref[...]
    base = CONV_HALO - (CONV_WIDTH - 1)
    phase_offs = [[o for o in range(base, base + CONV_WIDTH) if o % SUBLANES == ph] for ph in range(SUBLANES)]
    mix = (jnp.dot(omla_ref[...], wout_ref[:n_mla, :], preferred_element_type=F32)
           + jnp.dot(osb_ref[...], wout_ref[n_mla + CONV_CH:, :], preferred_element_type=F32))
    y = jnp.zeros((tm, CONV_CH), F32) + bdw_ref[...]
    for phase, offs in enumerate(phase_offs):
        rows = tm + offs[-1] - phase
        shift_ref[phase, 0:rows, :] = gext_ref[pl.ds(phase, rows), :]
        for o in offs:
            y = y + shift_ref[phase, o - phase:o - phase + tm, :] * wdw_ref[o - base:o - base + 1, :]
    mu = jnp.mean(y, axis=-1, keepdims=True)
    d = y - mu
    var = jnp.mean(d * d, axis=-1, keepdims=True)
    yn = d * lax.rsqrt(var + EPS) * gln_ref[...] + bln_ref[...]
    act = (yn * jax.nn.sigmoid(yn)).astype(BF16)
    oconv = jnp.dot(act, wpw_ref[...], preferred_element_type=F32).astype(BF16)
    mix = mix + jnp.dot(oconv, wout_ref[n_mla:n_mla + CONV_CH, :], preferred_element_type=F32)
    o_ref[...] = x_ref[...] + _rms(mix, gpost_ref[...])


def _mix_out(x, o_mla, g, o_sb, w_dw, b_dw, g_ln, b_ln, w_pw, w_out, g_post, seq, layer, tm=512):
    t = x.shape[0]
    tiles_per_seq = seq // tm
    halo_blocks = tm // CONV_HALO
    row = lambda n: pl.BlockSpec((tm, n), lambda i: (i, 0))
    prev = pl.BlockSpec((CONV_HALO, CONV_CH), lambda i: (jnp.maximum(i * halo_blocks - 1, 0), 0))
    return pl.pallas_call(
        functools.partial(_mix_out_kernel, tm=tm, tiles_per_seq=tiles_per_seq),
        grid=(t // tm,),
        in_specs=[
            row(D_MODEL), row(MLA_HEADS * MLA_V), row(CONV_CH), prev, row(SB_HEADS * SB_HEAD_DIM),
            _layer_mat((CONV_WIDTH, CONV_CH), layer), _layer_vec(CONV_CH, layer), _layer_vec(CONV_CH, layer),
            _layer_vec(CONV_CH, layer), _layer_mat((CONV_CH, CONV_CH), layer),
            _layer_mat((D_MIX, D_MODEL), layer), _layer_vec(D_MODEL, layer),
        ],
        out_specs=row(D_MODEL),
        out_shape=jax.ShapeDtypeStruct((t, D_MODEL), F32),
        scratch_shapes=[pltpu.VMEM((tm + CONV_HALO, CONV_CH), F32),
                        pltpu.VMEM((SUBLANES, tm + CONV_HALO, CONV_CH), F32)],
        compiler_params=_params("parallel"),
        name="mix_out",
    )(x, o_mla, g, g, o_sb, w_dw, b_dw, g_ln, b_ln, w_pw, w_out, g_post)


def _ple_kernel(x_ref, p_ref, gpre_ref, wg_ref, wp_ref, gpost_ref, o_ref):
    x = x_ref[...]
    h = _rms(x, gpre_ref[...]).astype(BF16)
    gate = jax.nn.sigmoid(jnp.dot(h, wg_ref[...], preferred_element_type=F32))
    e = jnp.dot(p_ref[...].astype(BF16), wp_ref[...], preferred_element_type=F32)
    o_ref[...] = x + _rms(gate * e, gpost_ref[...])


def _ple(x, p, g_pre, w_gate, w_proj, g_post, layer, tm=512):
    t = x.shape[0]
    row = lambda n: pl.BlockSpec((tm, n), lambda i: (i, 0))
    return pl.pallas_call(
        _ple_kernel,
        grid=(t // tm,),
        in_specs=[row(D_MODEL), pl.BlockSpec((None, tm, D_PLE), lambda i: (layer, i, 0)),
                  _layer_vec(D_MODEL, layer), _layer_mat((D_MODEL, D_MODEL), layer),
                  _layer_mat((D_PLE, D_MODEL), layer), _layer_vec(D_MODEL, layer)],
        out_specs=row(D_MODEL),
        out_shape=jax.ShapeDtypeStruct((t, D_MODEL), F32),
        compiler_params=_params("parallel"),
        name="ple",
    )(x, p, g_pre, w_gate, w_proj, g_post)


def _prep_w_in(w_in):
    half = MLA_ROPE // 2
    kr = w_in[:, :, _U_KR:_U_KR + MLA_ROPE]
    kr_sw = jnp.concatenate([kr[:, :, half:], kr[:, :, :half]], axis=2)
    return jnp.concatenate([w_in[:, :, :_U_KR + MLA_ROPE], kr_sw, w_in[:, :, _U_KR + MLA_ROPE:]],
                           axis=2).astype(BF16)


def _prep_w_uq(w_uq):
    depth = w_uq.shape[0]
    half = MLA_ROPE // 2
    w = w_uq.reshape(depth, Q_LORA, MLA_HEADS, MLA_QK)
    nope = w[..., :MLA_NOPE].reshape(depth, Q_LORA, -1)
    rope = w[..., MLA_NOPE:]
    rope_sw = jnp.concatenate([rope[..., half:], rope[..., :half]], axis=-1)
    return jnp.concatenate([nope, rope.reshape(depth, Q_LORA, -1), rope_sw.reshape(depth, Q_LORA, -1)],
                           axis=2).astype(BF16)


def _prep_w_ukv(w_ukv):
    depth = w_ukv.shape[0]
    w = w_ukv.reshape(depth, KV_LORA, MLA_HEADS, MLA_NOPE + MLA_V)
    return jnp.concatenate([w[..., :MLA_NOPE].reshape(depth, KV_LORA, -1),
                            w[..., MLA_NOPE:].reshape(depth, KV_LORA, -1)], axis=2).astype(BF16)


def kernel(x, p, positions, g_ff1_pre, w_ff1_in, w_ff1_out, g_ff1_post, g_mix_pre, w_in, g_cq, w_uq, g_ckv, w_ukv, w_dw, b_dw, g_conv_ln, b_conv_ln, w_pw, w_out, g_mix_post, g_ff2_pre, w_ff2_in, w_ff2_out, g_ff2_post, g_ple_pre, w_ple_gate, w_ple_proj, g_ple_post):
    batch, seq, _ = x.shape
    depth = p.shape[0]
    t = batch * seq
    xt = x.reshape(t, D_MODEL)
    pt = p.reshape(depth, t, D_PLE)
    cos_t, sin_t = _rope_tables(positions)
    vec = lambda a: a[:, None, :]
    w_in_ext, w_uq_ext, w_ukv_ext = _prep_w_in(w_in), _prep_w_uq(w_uq), _prep_w_ukv(w_ukv)
    w_pw_b, w_out_b = w_pw.astype(BF16), w_out.astype(BF16)
    w_gate_b, w_proj_b = w_ple_gate.astype(BF16), w_ple_proj.astype(BF16)
    for i in range(depth):
        xt = _ffn(xt, vec(g_ff1_pre), w_ff1_in, w_ff1_out, vec(g_ff1_post), i)
        qn, qr, kn, kr, v, g, sq, sk, sv = _proj(
            xt, vec(g_mix_pre), w_in_ext, vec(g_cq), w_uq_ext, vec(g_ckv), w_ukv_ext, cos_t, sin_t, i)
        o_mla = _mla_attention(qn, qr, kn, kr, v, batch, seq)
        o_sb = _sb_attention(sq, sk, sv, batch, seq)
        xt = _mix_out(xt, o_mla, g, o_sb, w_dw, vec(b_dw), vec(g_conv_ln), vec(b_conv_ln),
                      w_pw_b, w_out_b, vec(g_mix_post), seq, i)
        xt = _ffn(xt, vec(g_ff2_pre), w_ff2_in, w_ff2_out, vec(g_ff2_post), i)
        xt = _ple(xt, pt, vec(g_ple_pre), w_gate_b, w_proj_b, vec(g_ple_post), i)
    return xt.reshape(batch, seq, D_MODEL)
```

```python
import functools

import jax
import jax.numpy as jnp
from jax import lax
from jax.experimental import pallas as pl
from jax.experimental.pallas import tpu as pltpu

F32 = jnp.float32
BF16 = jnp.bfloat16

D_MODEL = 2048
D_PLE = 256
D_FF = 5632
MLA_HEADS = 8
MLA_NOPE = 128
MLA_ROPE = 64
MLA_QK = MLA_NOPE + MLA_ROPE
MLA_V = 128
Q_LORA = 512
KV_LORA = 256
ROPE_THETA = 10000.0
CONV_CH = 512
CONV_WIDTH = 31
SB_HEADS = 4
SB_HEAD_DIM = 128
D_MIX = MLA_HEADS * MLA_V + CONV_CH + SB_HEADS * SB_HEAD_DIM
EPS = 1e-6
NEG = -1e30

LOG2E = 1.4426950408889634
LN2 = 0.6931471805599453
MLA_SCORE_SCALE = MLA_QK ** -0.5 * LOG2E

LANES = 128
SUBLANES = 8
MXU_DIM = 256
PRE_NORM_ROWS = 128
LAST_ROWS = 256
CONV_HALO = 32

_U_CQ = 0
_U_CKV = Q_LORA
_U_KR = Q_LORA + KV_LORA
_U_CONV = _U_KR + 2 * MLA_ROPE
_U_SB = _U_CONV + 2 * CONV_CH
_U_END = _U_SB + 3 * SB_HEADS * SB_HEAD_DIM

VMEM_LIMIT = 48 * 1024 * 1024
FFN_VMEM_LIMIT = 58 * 1024 * 1024


def _params(*sem, vmem=VMEM_LIMIT):
    return pltpu.CompilerParams(dimension_semantics=sem, vmem_limit_bytes=vmem)


def _resident(shape):
    nd = len(shape)
    return pl.BlockSpec(shape, lambda *_: (0,) * nd, pipeline_mode=pl.Buffered(1))


def _layer_vec(n, layer):
    return pl.BlockSpec((None, 1, n), lambda *_: (layer, 0, 0), pipeline_mode=pl.Buffered(1))


def _layer_mat(shape, layer):
    return pl.BlockSpec((None,) + tuple(shape), lambda *_: (layer, 0, 0), pipeline_mode=pl.Buffered(1))


def _rms(x, g):
    return x * lax.rsqrt(jnp.mean(x * x, axis=-1, keepdims=True) + EPS) * g


def _rope_table_kernel(pos_ref, freq_ref, cos_ref, sin_ref):
    ang = pos_ref[...].astype(F32) * freq_ref[...]
    lane = lax.broadcasted_iota(jnp.int32, ang.shape, 1)
    first_half = (lane & (MLA_ROPE - 1)) < (MLA_ROPE // 2)
    cos_ref[...] = jnp.cos(ang)
    sin_ref[...] = jnp.where(first_half, -jnp.sin(ang), jnp.sin(ang))


def _rope_tables(positions, tm=1024):
    t = positions.size
    half = MLA_ROPE // 2
    inv_freq = ROPE_THETA ** (-jnp.arange(0, MLA_ROPE, 2, dtype=F32) / MLA_ROPE)
    freq = jnp.tile(inv_freq, LANES // half)[None, :]
    pos = positions.reshape(t, 1)
    return pl.pallas_call(
        _rope_table_kernel,
        grid=(t // tm,),
        in_specs=[pl.BlockSpec((tm, 1), lambda i: (i, 0)), _resident((1, LANES))],
        out_specs=[pl.BlockSpec((tm, LANES), lambda i: (i, 0))] * 2,
        out_shape=[jax.ShapeDtypeStruct((t, LANES), F32)] * 2,
        compiler_params=_params("parallel"),
        name="rope_tables",
    )(pos, freq)


def _ffn_kernel(x_ref, gpre_ref, wa_ref, wu_ref, wo_ref, gpost_ref, o_ref, h_ref, *, n_chunk):
    j = pl.program_id(1)

    tm = x_ref.shape[0]

    last = pl.num_programs(1) - 1

    def activation():
        h = h_ref[...]
        a = jnp.dot(h, wa_ref[...].astype(BF16), preferred_element_type=F32)
        u = jnp.dot(h, wu_ref[...].astype(BF16), preferred_element_type=F32)
        return (a * jax.nn.sigmoid(a) * u).astype(BF16)

    def accumulate(first):
        act = activation()
        for n in range(0, D_MODEL, n_chunk):
            part = jnp.dot(act, wo_ref[:, n:n + n_chunk].astype(BF16), preferred_element_type=F32)
            if first:
                o_ref[:, n:n + n_chunk] = part
            else:
                o_ref[:, n:n + n_chunk] += part

    @pl.when(j == 0)
    def _():
        for r in range(0, tm, PRE_NORM_ROWS):
            h_ref[r:r + PRE_NORM_ROWS, :] = _rms(x_ref[r:r + PRE_NORM_ROWS, :], gpre_ref[...]).astype(BF16)
        accumulate(True)

    @pl.when((j > 0) & (j < last))
    def _():
        accumulate(False)

    @pl.when(j == last)
    def _():
        act = activation()
        wo = wo_ref[...].astype(BF16)
        g_half = 0.5 * gpost_ref[...]
        for r in range(0, tm, LAST_ROWS):
            rows = slice(r, r + LAST_ROWS)
            acc = o_ref[rows, :] + jnp.dot(act[rows, :], wo, preferred_element_type=F32)
            o_ref[rows, :] = x_ref[rows, :] + _rms(acc, g_half)


def _ffn(x, g_pre, w_in, w_out, g_post, layer, tm=1024, tf=256, n_chunk=512):
    t = x.shape[0]
    nf = D_FF // tf
    return pl.pallas_call(
        functools.partial(_ffn_kernel, n_chunk=n_chunk),
        grid=(t // tm, nf),
        in_specs=[
            pl.BlockSpec((tm, D_MODEL), lambda i, j: (i, 0)),
            _layer_vec(D_MODEL, layer),
            pl.BlockSpec((None, D_MODEL, tf), lambda i, j: (layer, 0, j)),
            pl.BlockSpec((None, D_MODEL, tf), lambda i, j: (layer, 0, nf + j)),
            pl.BlockSpec((None, tf, D_MODEL), lambda i, j: (layer, j, 0)),
            _layer_vec(D_MODEL, layer),
        ],
        out_specs=pl.BlockSpec((tm, D_MODEL), lambda i, j: (i, 0)),
        out_shape=jax.ShapeDtypeStruct((t, D_MODEL), F32),
        scratch_shapes=[pltpu.VMEM((tm, D_MODEL), BF16)],
        compiler_params=_params("parallel", "arbitrary", vmem=FFN_VMEM_LIMIT),
        name="ffn",
    )(x, g_pre, w_in, w_in, w_out, g_post)


def _proj_kernel(x_ref, gpre_ref, win_ref, gcq_ref, wuq_ref, gckv_ref, wukv_ref, cos_ref, sin_ref,
                 qn_ref, qr_ref, kn_ref, kr_ref, v_ref, g_ref, sq_ref, sk_ref, sv_ref):
    h = _rms(x_ref[...], gpre_ref[...]).astype(BF16)
    u = jnp.dot(h, win_ref[...], preferred_element_type=F32)
    cos = cos_ref[...]
    sin = sin_ref[...]
    lane = lax.broadcasted_iota(jnp.int32, cos.shape, 1)
    upper = lane >= MLA_ROPE

    cq = _rms(u[:, _U_CQ:_U_CKV], gcq_ref[...]).astype(BF16)
    q = jnp.dot(cq, wuq_ref[...], preferred_element_type=F32) * MLA_SCORE_SCALE
    n_nope = MLA_HEADS * MLA_NOPE
    n_rope = MLA_HEADS * MLA_ROPE
    qn_ref[...] = q[:, :n_nope].astype(BF16)
    for pair in range(MLA_HEADS // 2):
        lo = n_nope + pair * LANES
        roped = q[:, lo:lo + LANES] * cos + q[:, lo + n_rope:lo + n_rope + LANES] * sin
        qr_ref[:, (2 * pair) * LANES:(2 * pair + 1) * LANES] = jnp.where(upper, 0.0, roped).astype(BF16)
        qr_ref[:, (2 * pair + 1) * LANES:(2 * pair + 2) * LANES] = jnp.where(upper, roped, 0.0).astype(BF16)

    ckv = _rms(u[:, _U_CKV:_U_KR], gckv_ref[...]).astype(BF16)
    kv = jnp.dot(ckv, wukv_ref[...], preferred_element_type=F32)
    kn_ref[...] = kv[:, :n_nope].astype(BF16)
    v_ref[...] = kv[:, n_nope:].astype(BF16)
    t = u[:, _U_KR:_U_CONV] * jnp.where(upper, sin, cos)
    kr_ref[...] = (t + pltpu.roll(t, MLA_ROPE, 1)).astype(BF16)

    a = u[:, _U_CONV:_U_CONV + CONV_CH]
    gate = u[:, _U_CONV + CONV_CH:_U_SB]
    g_ref[...] = a * jax.nn.sigmoid(gate)

    n_sb = SB_HEADS * SB_HEAD_DIM
    sq_ref[...] = (u[:, _U_SB:_U_SB + n_sb] * SB_HEAD_DIM ** -0.5).astype(BF16)
    sk_ref[...] = u[:, _U_SB + n_sb:_U_SB + 2 * n_sb].astype(BF16)
    sv_ref[...] = u[:, _U_SB + 2 * n_sb:_U_END].astype(BF16)


def _proj(x, g_pre, w_in_ext, g_cq, w_uq_ext, g_ckv, w_ukv_ext, cos_t, sin_t, layer, tm=256):
    t = x.shape[0]
    n_nope = MLA_HEADS * MLA_NOPE
    n_sb = SB_HEADS * SB_HEAD_DIM
    row = lambda n: pl.BlockSpec((tm, n), lambda i: (i, 0))
    out_cols = [(n_nope, BF16), (n_nope, BF16), (n_nope, BF16), (LANES, BF16), (n_nope, BF16),
                (CONV_CH, F32), (n_sb, BF16), (n_sb, BF16), (n_sb, BF16)]
    return pl.pallas_call(
        _proj_kernel,
        grid=(t // tm,),
        in_specs=[
            row(D_MODEL), _layer_vec(D_MODEL, layer), _layer_mat(w_in_ext.shape[1:], layer),
            _layer_vec(Q_LORA, layer), _layer_mat(w_uq_ext.shape[1:], layer),
            _layer_vec(KV_LORA, layer), _layer_mat(w_ukv_ext.shape[1:], layer),
            row(LANES), row(LANES),
        ],
        out_specs=[row(n) for n, _ in out_cols],
        out_shape=[jax.ShapeDtypeStruct((t, n), dt) for n, dt in out_cols],
        compiler_params=_params("parallel"),
        name="mixer_proj",
    )(x, g_pre, w_in_ext, g_cq, w_uq_ext, g_ckv, w_ukv_ext, cos_t, sin_t)


def _head(h):
    return slice(h * LANES, (h + 1) * LANES)


def _mla_kernel(qn_ref, qr_ref, kn_ref, kr_ref, v_ref, o_ref, m_ref, l_ref, acc_ref, *, tile, heads):
    qi = pl.program_id(2)
    m_ref[...] = jnp.full(m_ref.shape, NEG, F32)
    l_ref[...] = jnp.zeros(l_ref.shape, F32)
    acc_ref[...] = jnp.zeros(acc_ref.shape, F32)
    q_rows = qn_ref.shape[0]
    k_per_q = q_rows // tile

    def step(kj, r0, key_shift):
        rows = slice(r0, q_rows)
        ks = pl.multiple_of(kj * tile, tile)
        kr = kr_ref[pl.ds(ks, tile), :]
        for h in range(heads):
            q = jnp.concatenate([qn_ref[rows, _head(h)], qr_ref[rows, _head(h)]], axis=-1)
            k = jnp.concatenate([kn_ref[pl.ds(ks, tile), _head(h)], kr], axis=-1)
            s = lax.dot_general(q, k, (((1,), (1,)), ((), ())), preferred_element_type=F32)
            if key_shift is not None:
                row = lax.broadcasted_iota(jnp.int32, s.shape, 0) + r0
                col = lax.broadcasted_iota(jnp.int32, s.shape, 1) + key_shift
                s = jnp.where(col <= row, s, NEG)
            m_prev = m_ref[h, rows]
            m_new = jnp.maximum(m_prev, jnp.max(s, axis=-1, keepdims=True))
            alpha = jnp.exp2(m_prev - m_new)
            p = jnp.exp2(s - jnp.concatenate([m_new] * (tile // LANES), axis=-1))
            l_ref[h, rows] = alpha * l_ref[h, rows] + jnp.sum(p, axis=-1, keepdims=True)
            pv = jnp.dot(p.astype(BF16), v_ref[pl.ds(ks, tile), _head(h)], preferred_element_type=F32)
            acc_ref[h, rows] = alpha * acc_ref[h, rows] + pv
            m_ref[h, rows] = m_new

    def body(kj, carry):
        step(kj, 0, None)
        return carry

    lax.fori_loop(0, qi * k_per_q, body, 0)
    for d in range(k_per_q):
        step(qi * k_per_q + d, d * tile, d * tile)
    for h in range(heads):
        o_ref[:, _head(h)] = (acc_ref[h] / l_ref[h]).astype(BF16)


def _mla_attention(qn, qr, kn, kr, v, batch, seq, tile=512, q_rows=1024, heads=4):
    t = qn.shape[0]
    nq = seq // q_rows
    qspec = pl.BlockSpec((q_rows, heads * LANES), lambda b, h, i: (b * nq + i, h))
    kspec = pl.BlockSpec((seq, heads * LANES), lambda b, h, i: (b, h))
    return pl.pallas_call(
        functools.partial(_mla_kernel, tile=tile, heads=heads),
        grid=(batch, MLA_HEADS // heads, nq),
        in_specs=[qspec, qspec, kspec, pl.BlockSpec((seq, LANES), lambda b, h, i: (b, 0)), kspec],
        out_specs=qspec,
        out_shape=jax.ShapeDtypeStruct((t, MLA_HEADS * MLA_V), BF16),
        scratch_shapes=[pltpu.VMEM((heads, q_rows, LANES), F32)] * 3,
        compiler_params=_params("parallel", "parallel", "arbitrary"),
        name="mla_attention",
    )(qn, qr, kn, kr, v)


def _sb_kernel(q_ref, k_ref, v_ref, o_ref, run_ref, acc_ref, *, tile, heads):
    qi = pl.program_id(2)
    blk = MXU_DIM
    uj = lax.broadcasted_iota(jnp.int32, (blk, blk), 0)
    us = lax.broadcasted_iota(jnp.int32, (blk, blk), 1)
    suffix = jnp.where(uj > us, 1.0, 0.0).astype(BF16)
    run_ref[...] = jnp.zeros(run_ref.shape, F32)
    acc_ref[...] = jnp.zeros(acc_ref.shape, F32)

    def step(kj, diagonal):
        ks = pl.multiple_of(kj * tile, tile)
        for h in range(heads):
            z = lax.dot_general(q_ref[:, _head(h)], k_ref[pl.ds(ks, tile), _head(h)],
                                (((1,), (1,)), ((), ())), preferred_element_type=F32)
            softplus = jnp.log(1.0 + jnp.exp2(jnp.abs(z) * -LOG2E))
            log_beta = jnp.minimum(z, 0.0) - softplus
            log_keep = log_beta - z
            if diagonal:
                row = lax.broadcasted_iota(jnp.int32, z.shape, 0)
                col = lax.broadcasted_iota(jnp.int32, z.shape, 1)
                strict = col < row
                log_keep = jnp.where(strict, log_keep, 0.0)
            run = run_ref[h]
            after = [None] * (tile // blk)
            for c in reversed(range(tile // blk)):
                part = log_keep[:, c * blk:(c + 1) * blk]
                sums = jnp.dot(part.astype(BF16), suffix, preferred_element_type=F32)
                after[c] = sums + jnp.concatenate([run] * (blk // LANES), axis=-1)
                run = run + (sums[:, 0:1] + part[:, 0:1])
            w = jnp.exp(log_beta + jnp.concatenate(after, axis=-1))
            if diagonal:
                w = jnp.where(strict, w, 0.0)
            acc_ref[h] += jnp.dot(w.astype(BF16), v_ref[pl.ds(ks, tile), _head(h)],
                                  preferred_element_type=F32)
            run_ref[h] = run

    step(qi, True)

    def body(i, carry):
        step(qi - 1 - i, False)
        return carry

    lax.fori_loop(0, qi, body, 0)
    for h in range(heads):
        o_ref[:, _head(h)] = acc_ref[h].astype(BF16)


def _sb_attention(q, k, v, batch, seq, tile=512, heads=4):
    t = q.shape[0]
    nq = seq // tile
    qspec = pl.BlockSpec((tile, heads * LANES), lambda b, h, i: (b * nq + i, h))
    kspec = pl.BlockSpec((seq, heads * LANES), lambda b, h, i: (b, h))
    stat = pltpu.VMEM((heads, tile, LANES), F32)
    return pl.pallas_call(
        functools.partial(_sb_kernel, tile=tile, heads=heads),
        grid=(batch, SB_HEADS // heads, nq),
        in_specs=[qspec, kspec, kspec],
        out_specs=qspec,
        out_shape=jax.ShapeDtypeStruct((t, SB_HEADS * SB_HEAD_DIM), BF16),
        scratch_shapes=[stat, stat],
        compiler_params=_params("parallel", "parallel", "arbitrary"),
        name="sb_attention",
    )(q, k, v)


def _mix_out_kernel(x_ref, omla_ref, g_ref, gprev_ref, osb_ref, wdw_ref, bdw_ref, gln_ref, bln_ref,
                    wpw_ref, wout_ref, gpost_ref, o_ref, gext_ref, shift_ref, *, tm, tiles_per_seq):
    i = pl.program_id(0)
    n_mla = MLA_HEADS * MLA_V
    first = (i % tiles_per_seq) == 0
    gext_ref[0:CONV_HALO, :] = jnp.where(first, 0.0, gprev_ref[...])
    gext_ref[CONV_HALO:, :] = g_ref[...]
    base = CONV_HALO - (CONV_WIDTH - 1)
    phase_offs = [[o for o in range(base, base + CONV_WIDTH) if o % SUBLANES == ph] for ph in range(SUBLANES)]
    mix = (jnp.dot(omla_ref[...], wout_ref[:n_mla, :], preferred_element_type=F32)
           + jnp.dot(osb_ref[...], wout_ref[n_mla + CONV_CH:, :], preferred_element_type=F32))
    y = jnp.zeros((tm, CONV_CH), F32) + bdw_ref[...]
    for phase, offs in enumerate(phase_offs):
        rows = tm + offs[-1] - phase
        shift_ref[phase, 0:rows, :] = gext_ref[pl.ds(phase, rows), :]
        for o in offs:
            y = y + shift_ref[phase, o - phase:o - phase + tm, :] * wdw_ref[o - base:o - base + 1, :]
    mu = jnp.mean(y, axis=-1, keepdims=True)
    d = y - mu
    var = jnp.mean(d * d, axis=-1, keepdims=True)
    yn = d * lax.rsqrt(var + EPS) * gln_ref[...] + bln_ref[...]
    act = (yn * jax.nn.sigmoid(yn)).astype(BF16)
    oconv = jnp.dot(act, wpw_ref[...], preferred_element_type=F32).astype(BF16)
    mix = mix + jnp.dot(oconv, wout_ref[n_mla:n_mla + CONV_CH, :], preferred_element_type=F32)
    o_ref[...] = x_ref[...] + _rms(mix, gpost_ref[...])


def _mix_out(x, o_mla, g, o_sb, w_dw, b_dw, g_ln, b_ln, w_pw, w_out, g_post, seq, layer, tm=512):
    t = x.shape[0]
    tiles_per_seq = seq // tm
    halo_blocks = tm // CONV_HALO
    row = lambda n: pl.BlockSpec((tm, n), lambda i: (i, 0))
    prev = pl.BlockSpec((CONV_HALO, CONV_CH), lambda i: (jnp.maximum(i * halo_blocks - 1, 0), 0))
    return pl.pallas_call(
        functools.partial(_mix_out_kernel, tm=tm, tiles_per_seq=tiles_per_seq),
        grid=(t // tm,),
        in_specs=[
            row(D_MODEL), row(MLA_HEADS * MLA_V), row(CONV_CH), prev, row(SB_HEADS * SB_HEAD_DIM),
            _layer_mat((CONV_WIDTH, CONV_CH), layer), _layer_vec(CONV_CH, layer), _layer_vec(CONV_CH, layer),
            _layer_vec(CONV_CH, layer), _layer_mat((CONV_CH, CONV_CH), layer),
            _layer_mat((D_MIX, D_MODEL), layer), _layer_vec(D_MODEL, layer),
        ],
        out_specs=row(D_MODEL),
        out_shape=jax.ShapeDtypeStruct((t, D_MODEL), F32),
        scratch_shapes=[pltpu.VMEM((tm + CONV_HALO, CONV_CH), F32),
                        pltpu.VMEM((SUBLANES, tm + CONV_HALO, CONV_CH), F32)],
        compiler_params=_params("parallel"),
        name="mix_out",
    )(x, o_mla, g, g, o_sb, w_dw, b_dw, g_ln, b_ln, w_pw, w_out, g_post)


def _ple_kernel(x_ref, p_ref, gpre_ref, wg_ref, wp_ref, gpost_ref, o_ref):
    x = x_ref[...]
    h = _rms(x, gpre_ref[...]).astype(BF16)
    gate = jax.nn.sigmoid(jnp.dot(h, wg_ref[...], preferred_element_type=F32))
    e = jnp.dot(p_ref[...].astype(BF16), wp_ref[...], preferred_element_type=F32)
    o_ref[...] = x + _rms(gate * e, gpost_ref[...])


def _ple(x, p, g_pre, w_gate, w_proj, g_post, layer, tm=512):
    t = x.shape[0]
    row = lambda n: pl.BlockSpec((tm, n), lambda i: (i, 0))
    return pl.pallas_call(
        _ple_kernel,
        grid=(t // tm,),
        in_specs=[row(D_MODEL), pl.BlockSpec((None, tm, D_PLE), lambda i: (layer, i, 0)),
                  _layer_vec(D_MODEL, layer), _layer_mat((D_MODEL, D_MODEL), layer),
                  _layer_mat((D_PLE, D_MODEL), layer), _layer_vec(D_MODEL, layer)],
        out_specs=row(D_MODEL),
        out_shape=jax.ShapeDtypeStruct((t, D_MODEL), F32),
        compiler_params=_params("parallel"),
        name="ple",
    )(x, p, g_pre, w_gate, w_proj, g_post)


def _prep_w_in(w_in):
    half = MLA_ROPE // 2
    kr = w_in[:, :, _U_KR:_U_KR + MLA_ROPE]
    kr_sw = jnp.concatenate([kr[:, :, half:], kr[:, :, :half]], axis=2)
    return jnp.concatenate([w_in[:, :, :_U_KR + MLA_ROPE], kr_sw, w_in[:, :, _U_KR + MLA_ROPE:]],
                           axis=2).astype(BF16)


def _prep_w_uq(w_uq):
    depth = w_uq.shape[0]
    half = MLA_ROPE // 2
    w = w_uq.reshape(depth, Q_LORA, MLA_HEADS, MLA_QK)
    nope = w[..., :MLA_NOPE].reshape(depth, Q_LORA, -1)
    rope = w[..., MLA_NOPE:]
    rope_sw = jnp.concatenate([rope[..., half:], rope[..., :half]], axis=-1)
    return jnp.concatenate([nope, rope.reshape(depth, Q_LORA, -1), rope_sw.reshape(depth, Q_LORA, -1)],
                           axis=2).astype(BF16)


def _prep_w_ukv(w_ukv):
    depth = w_ukv.shape[0]
    w = w_ukv.reshape(depth, KV_LORA, MLA_HEADS, MLA_NOPE + MLA_V)
    return jnp.concatenate([w[..., :MLA_NOPE].reshape(depth, KV_LORA, -1),
                            w[..., MLA_NOPE:].reshape(depth, KV_LORA, -1)], axis=2).astype(BF16)


def kernel(x, p, positions, g_ff1_pre, w_ff1_in, w_ff1_out, g_ff1_post, g_mix_pre, w_in, g_cq, w_uq, g_ckv, w_ukv, w_dw, b_dw, g_conv_ln, b_conv_ln, w_pw, w_out, g_mix_post, g_ff2_pre, w_ff2_in, w_ff2_out, g_ff2_post, g_ple_pre, w_ple_gate, w_ple_proj, g_ple_post):
    batch, seq, _ = x.shape
    depth = p.shape[0]
    t = batch * seq
    xt = x.reshape(t, D_MODEL)
    pt = p.reshape(depth, t, D_PLE)
    cos_t, sin_t = _rope_tables(positions)
    vec = lambda a: a[:, None, :]
    w_in_ext, w_uq_ext, w_ukv_ext = _prep_w_in(w_in), _prep_w_uq(w_uq), _prep_w_ukv(w_ukv)
    w_pw_b, w_out_b = w_pw.astype(BF16), w_out.astype(BF16)
    w_gate_b, w_proj_b = w_ple_gate.astype(BF16), w_ple_proj.astype(BF16)
    for i in range(depth):
        xt = _ffn(xt, vec(g_ff1_pre), w_ff1_in, w_ff1_out, vec(g_ff1_post), i)
        qn, qr, kn, kr, v, g, sq, sk, sv = _proj(
            xt, vec(g_mix_pre), w_in_ext, vec(g_cq), w_uq_ext, vec(g_ckv), w_ukv_ext, cos_t, sin_t, i)
        o_mla = _mla_attention(qn, qr, kn, kr, v, batch, seq)
        o_sb = _sb_attention(sq, sk, sv, batch, seq)
        xt = _mix_out(xt, o_mla, g, o_sb, w_dw, vec(b_dw), vec(g_conv_ln), vec(b_conv_ln),
                      w_pw_b, w_out_b, vec(g_mix_post), seq, i)
        xt = _ffn(xt, vec(g_ff2_pre), w_ff2_in, w_ff2_out, vec(g_ff2_post), i)
        xt = _ple(xt, pt, vec(g_ple_pre), w_gate_b, w_proj_b, vec(g_ple_post), i)
    return xt.reshape(batch, seq, D_MODEL)
```

```python
import functools

import jax
import jax.numpy as jnp
from jax import lax
from jax.experimental import pallas as pl
from jax.experimental.pallas import tpu as pltpu

F32 = jnp.float32
BF16 = jnp.bfloat16

D_MODEL = 2048
D_PLE = 256
D_FF = 5632
MLA_HEADS = 8
MLA_NOPE = 128
MLA_ROPE = 64
MLA_QK = MLA_NOPE + MLA_ROPE
MLA_V = 128
Q_LORA = 512
KV_LORA = 256
ROPE_THETA = 10000.0
CONV_CH = 512
CONV_WIDTH = 31
SB_HEADS = 4
SB_HEAD_DIM = 128
D_MIX = MLA_HEADS * MLA_V + CONV_CH + SB_HEADS * SB_HEAD_DIM
EPS = 1e-6
NEG = -1e30

LOG2E = 1.4426950408889634
LN2 = 0.6931471805599453
MLA_SCORE_SCALE = MLA_QK ** -0.5 * LOG2E

LANES = 128
SUBLANES = 8
MXU_DIM = 256
PRE_NORM_ROWS = 128
LAST_ROWS = 256
CONV_HALO = 32

_U_CQ = 0
_U_CKV = Q_LORA
_U_KR = Q_LORA + KV_LORA
_U_CONV = _U_KR + 2 * MLA_ROPE
_U_SB = _U_CONV + 2 * CONV_CH
_U_END = _U_SB + 3 * SB_HEADS * SB_HEAD_DIM

VMEM_LIMIT = 48 * 1024 * 1024
FFN_VMEM_LIMIT = 58 * 1024 * 1024


def _params(*sem, vmem=VMEM_LIMIT):
    return pltpu.CompilerParams(dimension_semantics=sem, vmem_limit_bytes=vmem)


def _resident(shape):
    nd = len(shape)
    return pl.BlockSpec(shape, lambda *_: (0,) * nd, pipeline_mode=pl.Buffered(1))


def _layer_vec(n, layer):
    return pl.BlockSpec((None, 1, n), lambda *_: (layer, 0, 0), pipeline_mode=pl.Buffered(1))


def _layer_mat(shape, layer):
    return pl.BlockSpec((None,) + tuple(shape), lambda *_: (layer, 0, 0), pipeline_mode=pl.Buffered(1))


def _rms(x, g):
    return x * lax.rsqrt(jnp.mean(x * x, axis=-1, keepdims=True) + EPS) * g


def _rope_table_kernel(pos_ref, freq_ref, cos_ref, sin_ref):
    ang = pos_ref[...].astype(F32) * freq_ref[...]
    lane = lax.broadcasted_iota(jnp.int32, ang.shape, 1)
    first_half = (lane & (MLA_ROPE - 1)) < (MLA_ROPE // 2)
    cos_ref[...] = jnp.cos(ang)
    sin_ref[...] = jnp.where(first_half, -jnp.sin(ang), jnp.sin(ang))


def _rope_tables(positions, tm=1024):
    t = positions.size
    half = MLA_ROPE // 2
    inv_freq = ROPE_THETA ** (-jnp.arange(0, MLA_ROPE, 2, dtype=F32) / MLA_ROPE)
    freq = jnp.tile(inv_freq, LANES // half)[None, :]
    pos = positions.reshape(t, 1)
    return pl.pallas_call(
        _rope_table_kernel,
        grid=(t // tm,),
        in_specs=[pl.BlockSpec((tm, 1), lambda i: (i, 0)), _resident((1, LANES))],
        out_specs=[pl.BlockSpec((tm, LANES), lambda i: (i, 0))] * 2,
        out_shape=[jax.ShapeDtypeStruct((t, LANES), F32)] * 2,
        compiler_params=_params("parallel"),
        name="rope_tables",
    )(pos, freq)


def _ffn_kernel(x_ref, gpre_ref, wa_ref, wu_ref, wo_ref, gpost_ref, o_ref, h_ref, *, n_chunk):
    j = pl.program_id(1)

    tm = x_ref.shape[0]

    last = pl.num_programs(1) - 1

    def activation():
        h = h_ref[...]
        a = jnp.dot(h, wa_ref[...].astype(BF16), preferred_element_type=F32)
        u = jnp.dot(h, wu_ref[...].astype(BF16), preferred_element_type=F32)
        return (a * jax.nn.sigmoid(a) * u).astype(BF16)

    def accumulate(first):
        act = activation()
        for n in range(0, D_MODEL, n_chunk):
            part = jnp.dot(act, wo_ref[:, n:n + n_chunk].astype(BF16), preferred_element_type=F32)
            if first:
                o_ref[:, n:n + n_chunk] = part
            else:
                o_ref[:, n:n + n_chunk] += part

    @pl.when(j == 0)
    def _():
        for r in range(0, tm, PRE_NORM_ROWS):
            h_ref[r:r + PRE_NORM_ROWS, :] = _rms(x_ref[r:r + PRE_NORM_ROWS, :], gpre_ref[...]).astype(BF16)
        accumulate(True)

    @pl.when((j > 0) & (j < last))
    def _():
        accumulate(False)

    @pl.when(j == last)
    def _():
        act = activation()
        wo = wo_ref[...].astype(BF16)
        g_half = 0.5 * gpost_ref[...]
        for r in range(0, tm, LAST_ROWS):
            rows = slice(r, r + LAST_ROWS)
            acc = o_ref[rows, :] + jnp.dot(act[rows, :], wo, preferred_element_type=F32)
            o_ref[rows, :] = x_ref[rows, :] + _rms(acc, g_half)


def _ffn(x, g_pre, w_in, w_out, g_post, layer, tm=1024, tf=256, n_chunk=512):
    t = x.shape[0]
    nf = D_FF // tf
    return pl.pallas_call(
        functools.partial(_ffn_kernel, n_chunk=n_chunk),
        grid=(t // tm, nf),
        in_specs=[
            pl.BlockSpec((tm, D_MODEL), lambda i, j: (i, 0)),
            _layer_vec(D_MODEL, layer),
            pl.BlockSpec((None, D_MODEL, tf), lambda i, j: (layer, 0, j)),
            pl.BlockSpec((None, D_MODEL, tf), lambda i, j: (layer, 0, nf + j)),
            pl.BlockSpec((None, tf, D_MODEL), lambda i, j: (layer, j, 0)),
            _layer_vec(D_MODEL, layer),
        ],
        out_specs=pl.BlockSpec((tm, D_MODEL), lambda i, j: (i, 0)),
        out_shape=jax.ShapeDtypeStruct((t, D_MODEL), F32),
        scratch_shapes=[pltpu.VMEM((tm, D_MODEL), BF16)],
        compiler_params=_params("parallel", "arbitrary", vmem=FFN_VMEM_LIMIT),
        name="ffn",
    )(x, g_pre, w_in, w_in, w_out, g_post)


def _proj_kernel(x_ref, gpre_ref, win_ref, gcq_ref, wuq_ref, gckv_ref, wukv_ref, cos_ref, sin_ref,
                 qn_ref, qr_ref, kn_ref, kr_ref, v_ref, g_ref, sq_ref, sk_ref, sv_ref):
    h = _rms(x_ref[...], gpre_ref[...]).astype(BF16)
    u = jnp.dot(h, win_ref[...], preferred_element_type=F32)
    cos = cos_ref[...]
    sin = sin_ref[...]
    lane = lax.broadcasted_iota(jnp.int32, cos.shape, 1)
    upper = lane >= MLA_ROPE

    cq = _rms(u[:, _U_CQ:_U_CKV], gcq_ref[...]).astype(BF16)
    q = jnp.dot(cq, wuq_ref[...], preferred_element_type=F32) * MLA_SCORE_SCALE
    n_nope = MLA_HEADS * MLA_NOPE
    n_rope = MLA_HEADS * MLA_ROPE
    qn_ref[...] = q[:, :n_nope].astype(BF16)
    for pair in range(MLA_HEADS // 2):
        lo = n_nope + pair * LANES
        roped = q[:, lo:lo + LANES] * cos + q[:, lo + n_rope:lo + n_rope + LANES] * sin
        qr_ref[:, (2 * pair) * LANES:(2 * pair + 1) * LANES] = jnp.where(upper, 0.0, roped).astype(BF16)
        qr_ref[:, (2 * pair + 1) * LANES:(2 * pair + 2) * LANES] = jnp.where(upper, roped, 0.0).astype(BF16)

    ckv = _rms(u[:, _U_CKV:_U_KR], gckv_ref[...]).astype(BF16)
    kv = jnp.dot(ckv, wukv_ref[...], preferred_element_type=F32)
    kn_ref[...] = kv[:, :n_nope].astype(BF16)
    v_ref[...] = kv[:, n_nope:].astype(BF16)
    t = u[:, _U_KR:_U_CONV] * jnp.where(upper, sin, cos)
    kr_ref[...] = (t + pltpu.roll(t, MLA_ROPE, 1)).astype(BF16)

    a = u[:, _U_CONV:_U_CONV + CONV_CH]
    gate = u[:, _U_CONV + CONV_CH:_U_SB]
    g_ref[...] = a * jax.nn.sigmoid(gate)

    n_sb = SB_HEADS * SB_HEAD_DIM
    sq_ref[...] = (u[:, _U_SB:_U_SB + n_sb] * SB_HEAD_DIM ** -0.5).astype(BF16)
    sk_ref[...] = u[:, _U_SB + n_sb:_U_SB + 2 * n_sb].astype(BF16)
    sv_ref[...] = u[:, _U_SB + 2 * n_sb:_U_END].astype(BF16)


def _proj(x, g_pre, w_in_ext, g_cq, w_uq_ext, g_ckv, w_ukv_ext, cos_t, sin_t, layer, tm=512):
    t = x.shape[0]
    n_nope = MLA_HEADS * MLA_NOPE
    n_sb = SB_HEADS * SB_HEAD_DIM
    row = lambda n: pl.BlockSpec((tm, n), lambda i: (i, 0))
    out_cols = [(n_nope, BF16), (n_nope, BF16), (n_nope, BF16), (LANES, BF16), (n_nope, BF16),
                (CONV_CH, F32), (n_sb, BF16), (n_sb, BF16), (n_sb, BF16)]
    return pl.pallas_call(
        _proj_kernel,
        grid=(t // tm,),
        in_specs=[
            row(D_MODEL), _layer_vec(D_MODEL, layer), _layer_mat(w_in_ext.shape[1:], layer),
            _layer_vec(Q_LORA, layer), _layer_mat(w_uq_ext.shape[1:], layer),
            _layer_vec(KV_LORA, layer), _layer_mat(w_ukv_ext.shape[1:], layer),
            row(LANES), row(LANES),
        ],
        out_specs=[row(n) for n, _ in out_cols],
        out_shape=[jax.ShapeDtypeStruct((t, n), dt) for n, dt in out_cols],
        compiler_params=_params("parallel"),
        name="mixer_proj",
    )(x, g_pre, w_in_ext, g_cq, w_uq_ext, g_ckv, w_ukv_ext, cos_t, sin_t)


def _head(h):
    return slice(h * LANES, (h + 1) * LANES)


def _mla_kernel(qn_ref, qr_ref, kn_ref, kr_ref, v_ref, o_ref, m_ref, l_ref, acc_ref, *, tile, heads):
    qi = pl.program_id(2)
    m_ref[...] = jnp.full(m_ref.shape, NEG, F32)
    l_ref[...] = jnp.zeros(l_ref.shape, F32)
    acc_ref[...] = jnp.zeros(acc_ref.shape, F32)
    q_rows = qn_ref.shape[0]
    k_per_q = q_rows // tile

    def step(kj, r0, key_shift):
        rows = slice(r0, q_rows)
        ks = pl.multiple_of(kj * tile, tile)
        kr = kr_ref[pl.ds(ks, tile), :]
        for h in range(heads):
            q = jnp.concatenate([qn_ref[rows, _head(h)], qr_ref[rows, _head(h)]], axis=-1)
            k = jnp.concatenate([kn_ref[pl.ds(ks, tile), _head(h)], kr], axis=-1)
            s = lax.dot_general(q, k, (((1,), (1,)), ((), ())), preferred_element_type=F32)
            if key_shift is not None:
                row = lax.broadcasted_iota(jnp.int32, s.shape, 0) + r0
                col = lax.broadcasted_iota(jnp.int32, s.shape, 1) + key_shift
                s = jnp.where(col <= row, s, NEG)
            m_prev = m_ref[h, rows]
            m_new = jnp.maximum(m_prev, jnp.max(s, axis=-1, keepdims=True))
            alpha = jnp.exp2(m_prev - m_new)
            p = jnp.exp2(s - jnp.concatenate([m_new] * (tile // LANES), axis=-1))
            l_ref[h, rows] = alpha * l_ref[h, rows] + jnp.sum(p, axis=-1, keepdims=True)
            pv = jnp.dot(p.astype(BF16), v_ref[pl.ds(ks, tile), _head(h)], preferred_element_type=F32)
            acc_ref[h, rows] = alpha * acc_ref[h, rows] + pv
            m_ref[h, rows] = m_new

    def body(kj, carry):
        step(kj, 0, None)
        return carry

    lax.fori_loop(0, qi * k_per_q, body, 0)
    for d in range(k_per_q):
        step(qi * k_per_q + d, d * tile, d * tile)
    for h in range(heads):
        o_ref[:, _head(h)] = (acc_ref[h] / l_ref[h]).astype(BF16)


def _mla_attention(qn, qr, kn, kr, v, batch, seq, tile=512, q_rows=2048, heads=2):
    t = qn.shape[0]
    nq = seq // q_rows
    qspec = pl.BlockSpec((q_rows, heads * LANES), lambda b, h, i: (b * nq + i, h))
    kspec = pl.BlockSpec((seq, heads * LANES), lambda b, h, i: (b, h))
    return pl.pallas_call(
        functools.partial(_mla_kernel, tile=tile, heads=heads),
        grid=(batch, MLA_HEADS // heads, nq),
        in_specs=[qspec, qspec, kspec, pl.BlockSpec((seq, LANES), lambda b, h, i: (b, 0)), kspec],
        out_specs=qspec,
        out_shape=jax.ShapeDtypeStruct((t, MLA_HEADS * MLA_V), BF16),
        scratch_shapes=[pltpu.VMEM((heads, q_rows, LANES), F32)] * 3,
        compiler_params=_params("parallel", "parallel", "arbitrary"),
        name="mla_attention",
    )(qn, qr, kn, kr, v)


def _sb_kernel(q_ref, k_ref, v_ref, o_ref, run_ref, acc_ref, *, tile, heads):
    qi = pl.program_id(2)
    blk = MXU_DIM
    uj = lax.broadcasted_iota(jnp.int32, (blk, blk), 0)
    us = lax.broadcasted_iota(jnp.int32, (blk, blk), 1)
    suffix = jnp.where(uj > us, 1.0, 0.0).astype(BF16)
    run_ref[...] = jnp.zeros(run_ref.shape, F32)
    acc_ref[...] = jnp.zeros(acc_ref.shape, F32)
    q_rows = q_ref.shape[0]
    k_per_q = q_rows // tile

    def step(kj, r0, key_shift):
        rows = slice(r0, q_rows)
        ks = pl.multiple_of(kj * tile, tile)
        for h in range(heads):
            z = lax.dot_general(q_ref[rows, _head(h)], k_ref[pl.ds(ks, tile), _head(h)],
                                (((1,), (1,)), ((), ())), preferred_element_type=F32)
            softplus = jnp.log(1.0 + jnp.exp2(jnp.abs(z) * -LOG2E))
            log_beta = jnp.minimum(z, 0.0) - softplus
            log_keep = log_beta - z
            diagonal = key_shift is not None
            if diagonal:
                row = lax.broadcasted_iota(jnp.int32, z.shape, 0) + r0
                col = lax.broadcasted_iota(jnp.int32, z.shape, 1) + key_shift
                strict = col < row
                log_keep = jnp.where(strict, log_keep, 0.0)
            run = run_ref[h, rows]
            after = [None] * (tile // blk)
            for c in reversed(range(tile // blk)):
                part = log_keep[:, c * blk:(c + 1) * blk]
                sums = jnp.dot(part.astype(BF16), suffix, preferred_element_type=F32)
                after[c] = sums + jnp.concatenate([run] * (blk // LANES), axis=-1)
                run = run + (sums[:, 0:1] + part[:, 0:1])
            w = jnp.exp(log_beta + jnp.concatenate(after, axis=-1))
            if diagonal:
                w = jnp.where(strict, w, 0.0)
            acc_ref[h, rows] += jnp.dot(w.astype(BF16), v_ref[pl.ds(ks, tile), _head(h)],
                                        preferred_element_type=F32)
            run_ref[h, rows] = run

    for d in reversed(range(k_per_q)):
        step(qi * k_per_q + d, d * tile, d * tile)

    def body(i, carry):
        step(qi * k_per_q - 1 - i, 0, None)
        return carry

    lax.fori_loop(0, qi * k_per_q, body, 0)
    for h in range(heads):
        o_ref[:, _head(h)] = acc_ref[h].astype(BF16)


def _sb_attention(q, k, v, batch, seq, tile=512, q_rows=512, heads=4):
    t = q.shape[0]
    nq = seq // q_rows
    qspec = pl.BlockSpec((q_rows, heads * LANES), lambda b, h, i: (b * nq + i, h))
    kspec = pl.BlockSpec((seq, heads * LANES), lambda b, h, i: (b, h))
    stat = pltpu.VMEM((heads, q_rows, LANES), F32)
    return pl.pallas_call(
        functools.partial(_sb_kernel, tile=tile, heads=heads),
        grid=(batch, SB_HEADS // heads, nq),
        in_specs=[qspec, kspec, kspec],
        out_specs=qspec,
        out_shape=jax.ShapeDtypeStruct((t, SB_HEADS * SB_HEAD_DIM), BF16),
        scratch_shapes=[stat, stat],
        compiler_params=_params("parallel", "parallel", "arbitrary"),
        name="sb_attention",
    )(q, k, v)


def _mix_out_kernel(x_ref, omla_ref, g_ref, gprev_ref, osb_ref, wdw_ref, bdw_ref, gln_ref, bln_ref,
                    wpw_ref, wout_ref, gpost_ref, o_ref, gext_ref, shift_ref, *, tm, tiles_per_seq):
    i = pl.program_id(0)
    n_mla = MLA_HEADS * MLA_V
    first = (i % tiles_per_seq) == 0
    gext_ref[0:CONV_HALO, :] = jnp.where(first, 0.0, gprev_ref[...])
    gext_ref[CONV_HALO:, :] = g_ref[...]
    base = CONV_HALO - (CONV_WIDTH - 1)
    phase_offs = [[o for o in range(base, base + CONV_WIDTH) if o % SUBLANES == ph] for ph in range(SUBLANES)]
    mix = (jnp.dot(omla_ref[...], wout_ref[:n_mla, :], preferred_element_type=F32)
           + jnp.dot(osb_ref[...], wout_ref[n_mla + CONV_CH:, :], preferred_element_type=F32))
    y = jnp.zeros((tm, CONV_CH), F32) + bdw_ref[...]
    for phase, offs in enumerate(phase_offs):
        rows = tm + offs[-1] - phase
        shift_ref[phase, 0:rows, :] = gext_ref[pl.ds(phase, rows), :]
        for o in offs:
            y = y + shift_ref[phase, o - phase:o - phase + tm, :] * wdw_ref[o - base:o - base + 1, :]
    mu = jnp.mean(y, axis=-1, keepdims=True)
    d = y - mu
    var = jnp.mean(d * d, axis=-1, keepdims=True)
    yn = d * lax.rsqrt(var + EPS) * gln_ref[...] + bln_ref[...]
    act = (yn * jax.nn.sigmoid(yn)).astype(BF16)
    oconv = jnp.dot(act, wpw_ref[...], preferred_element_type=F32).astype(BF16)
    mix = mix + jnp.dot(oconv, wout_ref[n_mla:n_mla + CONV_CH, :], preferred_element_type=F32)
    o_ref[...] = x_ref[...] + _rms(mix, gpost_ref[...])


def _mix_out(x, o_mla, g, o_sb, w_dw, b_dw, g_ln, b_ln, w_pw, w_out, g_post, seq, layer, tm=512):
    t = x.shape[0]
    tiles_per_seq = seq // tm
    halo_blocks = tm // CONV_HALO
    row = lambda n: pl.BlockSpec((tm, n), lambda i: (i, 0))
    prev = pl.BlockSpec((CONV_HALO, CONV_CH), lambda i: (jnp.maximum(i * halo_blocks - 1, 0), 0))
    return pl.pallas_call(
        functools.partial(_mix_out_kernel, tm=tm, tiles_per_seq=tiles_per_seq),
        grid=(t // tm,),
        in_specs=[
            row(D_MODEL), row(MLA_HEADS * MLA_V), row(CONV_CH), prev, row(SB_HEADS * SB_HEAD_DIM),
            _layer_mat((CONV_WIDTH, CONV_CH), layer), _layer_vec(CONV_CH, layer), _layer_vec(CONV_CH, layer),
            _layer_vec(CONV_CH, layer), _layer_mat((CONV_CH, CONV_CH), layer),
            _layer_mat((D_MIX, D_MODEL), layer), _layer_vec(D_MODEL, layer),
        ],
        out_specs=row(D_MODEL),
        out_shape=jax.ShapeDtypeStruct((t, D_MODEL), F32),
        scratch_shapes=[pltpu.VMEM((tm + CONV_HALO, CONV_CH), F32),
                        pltpu.VMEM((SUBLANES, tm + CONV_HALO, CONV_CH), F32)],
        compiler_params=_params("parallel"),
        name="mix_out",
    )(x, o_mla, g, g, o_sb, w_dw, b_dw, g_ln, b_ln, w_pw, w_out, g_post)


def _ple_kernel(x_ref, p_ref, gpre_ref, wg_ref, wp_ref, gpost_ref, o_ref):
    x = x_ref[...]
    h = _rms(x, gpre_ref[...]).astype(BF16)
    gate = jax.nn.sigmoid(jnp.dot(h, wg_ref[...], preferred_element_type=F32))
    e = jnp.dot(p_ref[...].astype(BF16), wp_ref[...], preferred_element_type=F32)
    o_ref[...] = x + _rms(gate * e, gpost_ref[...])


def _ple(x, p, g_pre, w_gate, w_proj, g_post, layer, tm=512):
    t = x.shape[0]
    row = lambda n: pl.BlockSpec((tm, n), lambda i: (i, 0))
    return pl.pallas_call(
        _ple_kernel,
        grid=(t // tm,),
        in_specs=[row(D_MODEL), pl.BlockSpec((None, tm, D_PLE), lambda i: (layer, i, 0)),
                  _layer_vec(D_MODEL, layer), _layer_mat((D_MODEL, D_MODEL), layer),
                  _layer_mat((D_PLE, D_MODEL), layer), _layer_vec(D_MODEL, layer)],
        out_specs=row(D_MODEL),
        out_shape=jax.ShapeDtypeStruct((t, D_MODEL), F32),
        compiler_params=_params("parallel"),
        name="ple",
    )(x, p, g_pre, w_gate, w_proj, g_post)


def _prep_w_in(w_in):
    half = MLA_ROPE // 2
    kr = w_in[:, :, _U_KR:_U_KR + MLA_ROPE]
    kr_sw = jnp.concatenate([kr[:, :, half:], kr[:, :, :half]], axis=2)
    return jnp.concatenate([w_in[:, :, :_U_KR + MLA_ROPE], kr_sw, w_in[:, :, _U_KR + MLA_ROPE:]],
                           axis=2).astype(BF16)


def _prep_w_uq(w_uq):
    depth = w_uq.shape[0]
    half = MLA_ROPE // 2
    w = w_uq.reshape(depth, Q_LORA, MLA_HEADS, MLA_QK)
    nope = w[..., :MLA_NOPE].reshape(depth, Q_LORA, -1)
    rope = w[..., MLA_NOPE:]
    rope_sw = jnp.concatenate([rope[..., half:], rope[..., :half]], axis=-1)
    return jnp.concatenate([nope, rope.reshape(depth, Q_LORA, -1), rope_sw.reshape(depth, Q_LORA, -1)],
                           axis=2).astype(BF16)


def _prep_w_ukv(w_ukv):
    depth = w_ukv.shape[0]
    w = w_ukv.reshape(depth, KV_LORA, MLA_HEADS, MLA_NOPE + MLA_V)
    return jnp.concatenate([w[..., :MLA_NOPE].reshape(depth, KV_LORA, -1),
                            w[..., MLA_NOPE:].reshape(depth, KV_LORA, -1)], axis=2).astype(BF16)


def kernel(x, p, positions, g_ff1_pre, w_ff1_in, w_ff1_out, g_ff1_post, g_mix_pre, w_in, g_cq, w_uq, g_ckv, w_ukv, w_dw, b_dw, g_conv_ln, b_conv_ln, w_pw, w_out, g_mix_post, g_ff2_pre, w_ff2_in, w_ff2_out, g_ff2_post, g_ple_pre, w_ple_gate, w_ple_proj, g_ple_post):
    batch, seq, _ = x.shape
    depth = p.shape[0]
    t = batch * seq
    xt = x.reshape(t, D_MODEL)
    pt = p.reshape(depth, t, D_PLE)
    cos_t, sin_t = _rope_tables(positions)
    vec = lambda a: a[:, None, :]
    w_in_ext, w_uq_ext, w_ukv_ext = _prep_w_in(w_in), _prep_w_uq(w_uq), _prep_w_ukv(w_ukv)
    w_pw_b, w_out_b = w_pw.astype(BF16), w_out.astype(BF16)
    w_gate_b, w_proj_b = w_ple_gate.astype(BF16), w_ple_proj.astype(BF16)
    for i in range(depth):
        xt = _ffn(xt, vec(g_ff1_pre), w_ff1_in, w_ff1_out, vec(g_ff1_post), i)
        qn, qr, kn, kr, v, g, sq, sk, sv = _proj(
            xt, vec(g_mix_pre), w_in_ext, vec(g_cq), w_uq_ext, vec(g_ckv), w_ukv_ext, cos_t, sin_t, i)
        o_mla = _mla_attention(qn, qr, kn, kr, v, batch, seq)
        o_sb = _sb_attention(sq, sk, sv, batch, seq)
        xt = _mix_out(xt, o_mla, g, o_sb, w_dw, vec(b_dw), vec(g_conv_ln), vec(b_conv_ln),
                      w_pw_b, w_out_b, vec(g_mix_post), seq, i)
        xt = _ffn(xt, vec(g_ff2_pre), w_ff2_in, w_ff2_out, vec(g_ff2_post), i)
        xt = _ple(xt, pt, vec(g_ple_pre), w_gate_b, w_proj_b, vec(g_ple_post), i)
    return xt.reshape(batch, seq, D_MODEL)
```

```python
import functools

import jax
import jax.numpy as jnp
from jax import lax
from jax.experimental import pallas as pl
from jax.experimental.pallas import tpu as pltpu

F32 = jnp.float32
BF16 = jnp.bfloat16

D_MODEL = 2048
D_PLE = 256
D_FF = 5632
MLA_HEADS = 8
MLA_NOPE = 128
MLA_ROPE = 64
MLA_QK = MLA_NOPE + MLA_ROPE
MLA_V = 128
Q_LORA = 512
KV_LORA = 256
ROPE_THETA = 10000.0
CONV_CH = 512
CONV_WIDTH = 31
SB_HEADS = 4
SB_HEAD_DIM = 128
D_MIX = MLA_HEADS * MLA_V + CONV_CH + SB_HEADS * SB_HEAD_DIM
EPS = 1e-6
NEG = -1e30

LOG2E = 1.4426950408889634
LN2 = 0.6931471805599453
MLA_SCORE_SCALE = MLA_QK ** -0.5 * LOG2E

LANES = 128
SUBLANES = 8
MXU_DIM = 256
PRE_NORM_ROWS = 128
LAST_ROWS = 256
CONV_HALO = 32

_U_CQ = 0
_U_CKV = Q_LORA
_U_KR = Q_LORA + KV_LORA
_U_CONV = _U_KR + 2 * MLA_ROPE
_U_SB = _U_CONV + 2 * CONV_CH
_U_END = _U_SB + 3 * SB_HEADS * SB_HEAD_DIM

VMEM_LIMIT = 48 * 1024 * 1024
FFN_VMEM_LIMIT = 58 * 1024 * 1024


def _params(*sem, vmem=VMEM_LIMIT):
    return pltpu.CompilerParams(dimension_semantics=sem, vmem_limit_bytes=vmem)


def _resident(shape):
    nd = len(shape)
    return pl.BlockSpec(shape, lambda *_: (0,) * nd, pipeline_mode=pl.Buffered(1))


def _layer_vec(n, layer):
    return pl.BlockSpec((None, 1, n), lambda *_: (layer, 0, 0), pipeline_mode=pl.Buffered(1))


def _layer_mat(shape, layer):
    return pl.BlockSpec((None,) + tuple(shape), lambda *_: (layer, 0, 0), pipeline_mode=pl.Buffered(1))


def _rms(x, g):
    return x * lax.rsqrt(jnp.mean(x * x, axis=-1, keepdims=True) + EPS) * g


def _rope_table_kernel(pos_ref, freq_ref, cos_ref, sin_ref):
    ang = pos_ref[...].astype(F32) * freq_ref[...]
    lane = lax.broadcasted_iota(jnp.int32, ang.shape, 1)
    first_half = (lane & (MLA_ROPE - 1)) < (MLA_ROPE // 2)
    cos_ref[...] = jnp.cos(ang)
    sin_ref[...] = jnp.where(first_half, -jnp.sin(ang), jnp.sin(ang))


def _rope_tables(positions, tm=1024):
    t = positions.size
    half = MLA_ROPE // 2
    inv_freq = ROPE_THETA ** (-jnp.arange(0, MLA_ROPE, 2, dtype=F32) / MLA_ROPE)
    freq = jnp.tile(inv_freq, LANES // half)[None, :]
    pos = positions.reshape(t, 1)
    return pl.pallas_call(
        _rope_table_kernel,
        grid=(t // tm,),
        in_specs=[pl.BlockSpec((tm, 1), lambda i: (i, 0)), _resident((1, LANES))],
        out_specs=[pl.BlockSpec((tm, LANES), lambda i: (i, 0))] * 2,
        out_shape=[jax.ShapeDtypeStruct((t, LANES), F32)] * 2,
        compiler_params=_params("parallel"),
        name="rope_tables",
    )(pos, freq)


def _ffn_kernel(x_ref, gpre_ref, w_in_hbm, w_out_hbm, gpost_ref, o_ref, h_ref, *, layer, tf, n_chunk):
    tm = x_ref.shape[0]
    nf = D_FF // tf
    for r in range(0, tm, PRE_NORM_ROWS):
        h_ref[r:r + PRE_NORM_ROWS, :] = _rms(x_ref[r:r + PRE_NORM_ROWS, :], gpre_ref[...]).astype(BF16)
    o_ref[...] = jnp.zeros(o_ref.shape, F32)

    def hidden_tile(wa_ref, wu_ref, wo_ref):
        h = h_ref[...]
        a = jnp.dot(h, wa_ref[...].astype(BF16), preferred_element_type=F32)
        u = jnp.dot(h, wu_ref[...].astype(BF16), preferred_element_type=F32)
        act = (a * jax.nn.sigmoid(a) * u).astype(BF16)
        for n in range(0, D_MODEL, n_chunk):
            o_ref[:, n:n + n_chunk] += jnp.dot(act, wo_ref[:, n:n + n_chunk].astype(BF16),
                                               preferred_element_type=F32)

    w_in = w_in_hbm.at[layer]
    w_out = w_out_hbm.at[layer]
    pltpu.emit_pipeline(
        hidden_tile,
        grid=(nf,),
        in_specs=[
            pl.BlockSpec((D_MODEL, tf), lambda j: (0, j), pipeline_mode=pl.Buffered(3)),
            pl.BlockSpec((D_MODEL, tf), lambda j: (0, nf + j), pipeline_mode=pl.Buffered(3)),
            pl.BlockSpec((tf, D_MODEL), lambda j: (j, 0)),
        ],
    )(w_in, w_in, w_out)

    g_half = 0.5 * gpost_ref[...]
    for r in range(0, tm, LAST_ROWS):
        rows = slice(r, r + LAST_ROWS)
        o_ref[rows, :] = x_ref[rows, :] + _rms(o_ref[rows, :], g_half)


def _ffn(x, g_pre, w_in, w_out, g_post, layer, tm=1024, tf=256, n_chunk=512):
    t = x.shape[0]
    return pl.pallas_call(
        functools.partial(_ffn_kernel, layer=layer, tf=tf, n_chunk=n_chunk),
        grid=(t // tm,),
        in_specs=[
            pl.BlockSpec((tm, D_MODEL), lambda i: (i, 0)),
            _layer_vec(D_MODEL, layer),
            pl.BlockSpec(memory_space=pl.ANY),
            pl.BlockSpec(memory_space=pl.ANY),
            _layer_vec(D_MODEL, layer),
        ],
        out_specs=pl.BlockSpec((tm, D_MODEL), lambda i: (i, 0)),
        out_shape=jax.ShapeDtypeStruct((t, D_MODEL), F32),
        scratch_shapes=[pltpu.VMEM((tm, D_MODEL), BF16)],
        compiler_params=_params("arbitrary", vmem=FFN_VMEM_LIMIT),
        name="ffn",
    )(x, g_pre, w_in, w_out, g_post)


def _proj_kernel(x_ref, gpre_ref, win_ref, gcq_ref, wuq_ref, gckv_ref, wukv_ref, cos_ref, sin_ref,
                 qn_ref, qr_ref, kn_ref, kr_ref, v_ref, g_ref, sq_ref, sk_ref, sv_ref):
    h = _rms(x_ref[...], gpre_ref[...]).astype(BF16)
    u = jnp.dot(h, win_ref[...], preferred_element_type=F32)
    cos = cos_ref[...]
    sin = sin_ref[...]
    lane = lax.broadcasted_iota(jnp.int32, cos.shape, 1)
    upper = lane >= MLA_ROPE

    cq = _rms(u[:, _U_CQ:_U_CKV], gcq_ref[...]).astype(BF16)
    q = jnp.dot(cq, wuq_ref[...], preferred_element_type=F32) * MLA_SCORE_SCALE
    n_nope = MLA_HEADS * MLA_NOPE
    n_rope = MLA_HEADS * MLA_ROPE
    qn_ref[...] = q[:, :n_nope].astype(BF16)
    for pair in range(MLA_HEADS // 2):
        lo = n_nope + pair * LANES
        roped = q[:, lo:lo + LANES] * cos + q[:, lo + n_rope:lo + n_rope + LANES] * sin
        qr_ref[:, (2 * pair) * LANES:(2 * pair + 1) * LANES] = jnp.where(upper, 0.0, roped).astype(BF16)
        qr_ref[:, (2 * pair + 1) * LANES:(2 * pair + 2) * LANES] = jnp.where(upper, roped, 0.0).astype(BF16)

    ckv = _rms(u[:, _U_CKV:_U_KR], gckv_ref[...]).astype(BF16)
    kv = jnp.dot(ckv, wukv_ref[...], preferred_element_type=F32)
    kn_ref[...] = kv[:, :n_nope].astype(BF16)
    v_ref[...] = kv[:, n_nope:].astype(BF16)
    t = u[:, _U_KR:_U_CONV] * jnp.where(upper, sin, cos)
    kr_ref[...] = (t + pltpu.roll(t, MLA_ROPE, 1)).astype(BF16)

    a = u[:, _U_CONV:_U_CONV + CONV_CH]
    gate = u[:, _U_CONV + CONV_CH:_U_SB]
    g_ref[...] = a * jax.nn.sigmoid(gate)

    n_sb = SB_HEADS * SB_HEAD_DIM
    sq_ref[...] = (u[:, _U_SB:_U_SB + n_sb] * SB_HEAD_DIM ** -0.5).astype(BF16)
    sk_ref[...] = u[:, _U_SB + n_sb:_U_SB + 2 * n_sb].astype(BF16)
    sv_ref[...] = u[:, _U_SB + 2 * n_sb:_U_END].astype(BF16)


def _proj(x, g_pre, w_in_ext, g_cq, w_uq_ext, g_ckv, w_ukv_ext, cos_t, sin_t, layer, tm=512):
    t = x.shape[0]
    n_nope = MLA_HEADS * MLA_NOPE
    n_sb = SB_HEADS * SB_HEAD_DIM
    row = lambda n: pl.BlockSpec((tm, n), lambda i: (i, 0))
    out_cols = [(n_nope, BF16), (n_nope, BF16), (n_nope, BF16), (LANES, BF16), (n_nope, BF16),
                (CONV_CH, F32), (n_sb, BF16), (n_sb, BF16), (n_sb, BF16)]
    return pl.pallas_call(
        _proj_kernel,
        grid=(t // tm,),
        in_specs=[
            row(D_MODEL), _layer_vec(D_MODEL, layer), _layer_mat(w_in_ext.shape[1:], layer),
            _layer_vec(Q_LORA, layer), _layer_mat(w_uq_ext.shape[1:], layer),
            _layer_vec(KV_LORA, layer), _layer_mat(w_ukv_ext.shape[1:], layer),
            row(LANES), row(LANES),
        ],
        out_specs=[row(n) for n, _ in out_cols],
        out_shape=[jax.ShapeDtypeStruct((t, n), dt) for n, dt in out_cols],
        compiler_params=_params("parallel"),
        name="mixer_proj",
    )(x, g_pre, w_in_ext, g_cq, w_uq_ext, g_ckv, w_ukv_ext, cos_t, sin_t)


def _head(h):
    return slice(h * LANES, (h + 1) * LANES)


def _mla_kernel(qn_ref, qr_ref, kn_ref, kr_ref, v_ref, o_ref, m_ref, l_ref, acc_ref, *, tile, heads):
    qi = pl.program_id(2)
    m_ref[...] = jnp.full(m_ref.shape, NEG, F32)
    l_ref[...] = jnp.zeros(l_ref.shape, F32)
    acc_ref[...] = jnp.zeros(acc_ref.shape, F32)
    q_rows = qn_ref.shape[0]
    k_per_q = q_rows // tile

    def step(kj, r0, key_shift):
        rows = slice(r0, q_rows)
        ks = pl.multiple_of(kj * tile, tile)
        kr = kr_ref[pl.ds(ks, tile), :]
        for h in range(heads):
            q = jnp.concatenate([qn_ref[rows, _head(h)], qr_ref[rows, _head(h)]], axis=-1)
            k = jnp.concatenate([kn_ref[pl.ds(ks, tile), _head(h)], kr], axis=-1)
            s = lax.dot_general(q, k, (((1,), (1,)), ((), ())), preferred_element_type=F32)
            if key_shift is not None:
                row = lax.broadcasted_iota(jnp.int32, s.shape, 0) + r0
                col = lax.broadcasted_iota(jnp.int32, s.shape, 1) + key_shift
                s = jnp.where(col <= row, s, NEG)
            m_prev = m_ref[h, rows]
            m_new = jnp.maximum(m_prev, jnp.max(s, axis=-1, keepdims=True))
            alpha = jnp.exp2(m_prev - m_new)
            p = jnp.exp2(s - jnp.concatenate([m_new] * (tile // LANES), axis=-1))
            l_ref[h, rows] = alpha * l_ref[h, rows] + jnp.sum(p, axis=-1, keepdims=True)
            pv = jnp.dot(p.astype(BF16), v_ref[pl.ds(ks, tile), _head(h)], preferred_element_type=F32)
            acc_ref[h, rows] = alpha * acc_ref[h, rows] + pv
            m_ref[h, rows] = m_new

    def body(kj, carry):
        step(kj, 0, None)
        return carry

    lax.fori_loop(0, qi * k_per_q, body, 0)
    for d in range(k_per_q):
        step(qi * k_per_q + d, d * tile, d * tile)
    for h in range(heads):
        o_ref[:, _head(h)] = (acc_ref[h] / l_ref[h]).astype(BF16)


def _mla_attention(qn, qr, kn, kr, v, batch, seq, tile=512, q_rows=2048, heads=2):
    t = qn.shape[0]
    nq = seq // q_rows
    qspec = pl.BlockSpec((q_rows, heads * LANES), lambda b, h, i: (b * nq + i, h))
    kspec = pl.BlockSpec((seq, heads * LANES), lambda b, h, i: (b, h))
    return pl.pallas_call(
        functools.partial(_mla_kernel, tile=tile, heads=heads),
        grid=(batch, MLA_HEADS // heads, nq),
        in_specs=[qspec, qspec, kspec, pl.BlockSpec((seq, LANES), lambda b, h, i: (b, 0)), kspec],
        out_specs=qspec,
        out_shape=jax.ShapeDtypeStruct((t, MLA_HEADS * MLA_V), BF16),
        scratch_shapes=[pltpu.VMEM((heads, q_rows, LANES), F32)] * 3,
        compiler_params=_params("parallel", "parallel", "arbitrary"),
        name="mla_attention",
    )(qn, qr, kn, kr, v)


def _sb_kernel(q_ref, k_ref, v_ref, o_ref, run_ref, acc_ref, *, tile, heads):
    qi = pl.program_id(2)
    blk = MXU_DIM
    uj = lax.broadcasted_iota(jnp.int32, (blk, blk), 0)
    us = lax.broadcasted_iota(jnp.int32, (blk, blk), 1)
    suffix = jnp.where(uj > us, 1.0, 0.0).astype(BF16)
    run_ref[...] = jnp.zeros(run_ref.shape, F32)
    acc_ref[...] = jnp.zeros(acc_ref.shape, F32)
    q_rows = q_ref.shape[0]
    k_per_q = q_rows // tile

    def step(kj, r0, key_shift):
        rows = slice(r0, q_rows)
        ks = pl.multiple_of(kj * tile, tile)
        for h in range(heads):
            z = lax.dot_general(q_ref[rows, _head(h)], k_ref[pl.ds(ks, tile), _head(h)],
                                (((1,), (1,)), ((), ())), preferred_element_type=F32)
            softplus = jnp.log(1.0 + jnp.exp2(jnp.abs(z) * -LOG2E))
            log_beta = jnp.minimum(z, 0.0) - softplus
            log_keep = log_beta - z
            diagonal = key_shift is not None
            if diagonal:
                row = lax.broadcasted_iota(jnp.int32, z.shape, 0) + r0
                col = lax.broadcasted_iota(jnp.int32, z.shape, 1) + key_shift
                strict = col < row
                log_keep = jnp.where(strict, log_keep, 0.0)
            run = run_ref[h, rows]
            after = [None] * (tile // blk)
            for c in reversed(range(tile // blk)):
                part = log_keep[:, c * blk:(c + 1) * blk]
                sums = jnp.dot(part.astype(BF16), suffix, preferred_element_type=F32)
                after[c] = sums + jnp.concatenate([run] * (blk // LANES), axis=-1)
                run = run + (sums[:, 0:1] + part[:, 0:1])
            w = jnp.exp(log_beta + jnp.concatenate(after, axis=-1))
            if diagonal:
                w = jnp.where(strict, w, 0.0)
            acc_ref[h, rows] += jnp.dot(w.astype(BF16), v_ref[pl.ds(ks, tile), _head(h)],
                                        preferred_element_type=F32)
            run_ref[h, rows] = run

    for d in reversed(range(k_per_q)):
        step(qi * k_per_q + d, d * tile, d * tile)

    def body(i, carry):
        step(qi * k_per_q - 1 - i, 0, None)
        return carry

    lax.fori_loop(0, qi * k_per_q, body, 0)
    for h in range(heads):
        o_ref[:, _head(h)] = acc_ref[h].astype(BF16)


def _sb_attention(q, k, v, batch, seq, tile=512, q_rows=512, heads=4):
    t = q.shape[0]
    nq = seq // q_rows
    qspec = pl.BlockSpec((q_rows, heads * LANES), lambda b, h, i: (b * nq + i, h))
    kspec = pl.BlockSpec((seq, heads * LANES), lambda b, h, i: (b, h))
    stat = pltpu.VMEM((heads, q_rows, LANES), F32)
    return pl.pallas_call(
        functools.partial(_sb_kernel, tile=tile, heads=heads),
        grid=(batch, SB_HEADS // heads, nq),
        in_specs=[qspec, kspec, kspec],
        out_specs=qspec,
        out_shape=jax.ShapeDtypeStruct((t, SB_HEADS * SB_HEAD_DIM), BF16),
        scratch_shapes=[stat, stat],
        compiler_params=_params("parallel", "parallel", "arbitrary"),
        name="sb_attention",
    )(q, k, v)


def _mix_out_kernel(x_ref, omla_ref, g_ref, gprev_ref, osb_ref, wdw_ref, bdw_ref, gln_ref, bln_ref,
                    wpw_ref, wout_ref, gpost_ref, o_ref, gext_ref, shift_ref, *, tm, tiles_per_seq):
    i = pl.program_id(0)
    n_mla = MLA_HEADS * MLA_V
    first = (i % tiles_per_seq) == 0
    gext_ref[0:CONV_HALO, :] = jnp.where(first, 0.0, gprev_ref[...])
    gext_ref[CONV_HALO:, :] = g_ref[...]
    base = CONV_HALO - (CONV_WIDTH - 1)
    phase_offs = [[o for o in range(base, base + CONV_WIDTH) if o % SUBLANES == ph] for ph in range(SUBLANES)]
    mix = (jnp.dot(omla_ref[...], wout_ref[:n_mla, :], preferred_element_type=F32)
           + jnp.dot(osb_ref[...], wout_ref[n_mla + CONV_CH:, :], preferred_element_type=F32))
    y = jnp.zeros((tm, CONV_CH), F32) + bdw_ref[...]
    for phase, offs in enumerate(phase_offs):
        rows = tm + offs[-1] - phase
        shift_ref[phase, 0:rows, :] = gext_ref[pl.ds(phase, rows), :]
        for o in offs:
            y = y + shift_ref[phase, o - phase:o - phase + tm, :] * wdw_ref[o - base:o - base + 1, :]
    mu = jnp.mean(y, axis=-1, keepdims=True)
    d = y - mu
    var = jnp.mean(d * d, axis=-1, keepdims=True)
    yn = d * lax.rsqrt(var + EPS) * gln_ref[...] + bln_ref[...]
    act = (yn * jax.nn.sigmoid(yn)).astype(BF16)
    oconv = jnp.dot(act, wpw_ref[...], preferred_element_type=F32).astype(BF16)
    mix = mix + jnp.dot(oconv, wout_ref[n_mla:n_mla + CONV_CH, :], preferred_element_type=F32)
    o_ref[...] = x_ref[...] + _rms(mix, gpost_ref[...])


def _mix_out(x, o_mla, g, o_sb, w_dw, b_dw, g_ln, b_ln, w_pw, w_out, g_post, seq, layer, tm=512):
    t = x.shape[0]
    tiles_per_seq = seq // tm
    halo_blocks = tm // CONV_HALO
    row = lambda n: pl.BlockSpec((tm, n), lambda i: (i, 0))
    prev = pl.BlockSpec((CONV_HALO, CONV_CH), lambda i: (jnp.maximum(i * halo_blocks - 1, 0), 0))
    return pl.pallas_call(
        functools.partial(_mix_out_kernel, tm=tm, tiles_per_seq=tiles_per_seq),
        grid=(t // tm,),
        in_specs=[
            row(D_MODEL), row(MLA_HEADS * MLA_V), row(CONV_CH), prev, row(SB_HEADS * SB_HEAD_DIM),
            _layer_mat((CONV_WIDTH, CONV_CH), layer), _layer_vec(CONV_CH, layer), _layer_vec(CONV_CH, layer),
            _layer_vec(CONV_CH, layer), _layer_mat((CONV_CH, CONV_CH), layer),
            _layer_mat((D_MIX, D_MODEL), layer), _layer_vec(D_MODEL, layer),
        ],
        out_specs=row(D_MODEL),
        out_shape=jax.ShapeDtypeStruct((t, D_MODEL), F32),
        scratch_shapes=[pltpu.VMEM((tm + CONV_HALO, CONV_CH), F32),
                        pltpu.VMEM((SUBLANES, tm + CONV_HALO, CONV_CH), F32)],
        compiler_params=_params("parallel"),
        name="mix_out",
    )(x, o_mla, g, g, o_sb, w_dw, b_dw, g_ln, b_ln, w_pw, w_out, g_post)


def _ple_kernel(x_ref, p_ref, gpre_ref, wg_ref, wp_ref, gpost_ref, o_ref):
    x = x_ref[...]
    h = _rms(x, gpre_ref[...]).astype(BF16)
    gate = jax.nn.sigmoid(jnp.dot(h, wg_ref[...], preferred_element_type=F32))
    e = jnp.dot(p_ref[...].astype(BF16), wp_ref[...], preferred_element_type=F32)
    o_ref[...] = x + _rms(gate * e, gpost_ref[...])


def _ple(x, p, g_pre, w_gate, w_proj, g_post, layer, tm=512):
    t = x.shape[0]
    row = lambda n: pl.BlockSpec((tm, n), lambda i: (i, 0))
    return pl.pallas_call(
        _ple_kernel,
        grid=(t // tm,),
        in_specs=[row(D_MODEL), pl.BlockSpec((None, tm, D_PLE), lambda i: (layer, i, 0)),
                  _layer_vec(D_MODEL, layer), _layer_mat((D_MODEL, D_MODEL), layer),
                  _layer_mat((D_PLE, D_MODEL), layer), _layer_vec(D_MODEL, layer)],
        out_specs=row(D_MODEL),
        out_shape=jax.ShapeDtypeStruct((t, D_MODEL), F32),
        compiler_params=_params("parallel"),
        name="ple",
    )(x, p, g_pre, w_gate, w_proj, g_post)


def _prep_w_in(w_in):
    half = MLA_ROPE // 2
    kr = w_in[:, :, _U_KR:_U_KR + MLA_ROPE]
    kr_sw = jnp.concatenate([kr[:, :, half:], kr[:, :, :half]], axis=2)
    return jnp.concatenate([w_in[:, :, :_U_KR + MLA_ROPE], kr_sw, w_in[:, :, _U_KR + MLA_ROPE:]],
                           axis=2).astype(BF16)


def _prep_w_uq(w_uq):
    depth = w_uq.shape[0]
    half = MLA_ROPE // 2
    w = w_uq.reshape(depth, Q_LORA, MLA_HEADS, MLA_QK)
    nope = w[..., :MLA_NOPE].reshape(depth, Q_LORA, -1)
    rope = w[..., MLA_NOPE:]
    rope_sw = jnp.concatenate([rope[..., half:], rope[..., :half]], axis=-1)
    return jnp.concatenate([nope, rope.reshape(depth, Q_LORA, -1), rope_sw.reshape(depth, Q_LORA, -1)],
                           axis=2).astype(BF16)


def _prep_w_ukv(w_ukv):
    depth = w_ukv.shape[0]
    w = w_ukv.reshape(depth, KV_LORA, MLA_HEADS, MLA_NOPE + MLA_V)
    return jnp.concatenate([w[..., :MLA_NOPE].reshape(depth, KV_LORA, -1),
                            w[..., MLA_NOPE:].reshape(depth, KV_LORA, -1)], axis=2).astype(BF16)


def kernel(x, p, positions, g_ff1_pre, w_ff1_in, w_ff1_out, g_ff1_post, g_mix_pre, w_in, g_cq, w_uq, g_ckv, w_ukv, w_dw, b_dw, g_conv_ln, b_conv_ln, w_pw, w_out, g_mix_post, g_ff2_pre, w_ff2_in, w_ff2_out, g_ff2_post, g_ple_pre, w_ple_gate, w_ple_proj, g_ple_post):
    batch, seq, _ = x.shape
    depth = p.shape[0]
    t = batch * seq
    xt = x.reshape(t, D_MODEL)
    pt = p.reshape(depth, t, D_PLE)
    cos_t, sin_t = _rope_tables(positions)
    vec = lambda a: a[:, None, :]
    w_in_ext, w_uq_ext, w_ukv_ext = _prep_w_in(w_in), _prep_w_uq(w_uq), _prep_w_ukv(w_ukv)
    w_pw_b, w_out_b = w_pw.astype(BF16), w_out.astype(BF16)
    w_gate_b, w_proj_b = w_ple_gate.astype(BF16), w_ple_proj.astype(BF16)
    for i in range(depth):
        xt = _ffn(xt, vec(g_ff1_pre), w_ff1_in, w_ff1_out, vec(g_ff1_post), i)
        qn, qr, kn, kr, v, g, sq, sk, sv = _proj(
            xt, vec(g_mix_pre), w_in_ext, vec(g_cq), w_uq_ext, vec(g_ckv), w_ukv_ext, cos_t, sin_t, i)
        o_mla = _mla_attention(qn, qr, kn, kr, v, batch, seq)
        o_sb = _sb_attention(sq, sk, sv, batch, seq)
        xt = _mix_out(xt, o_mla, g, o_sb, w_dw, vec(b_dw), vec(g_conv_ln), vec(b_conv_ln),
                      w_pw_b, w_out_b, vec(g_mix_post), seq, i)
        xt = _ffn(xt, vec(g_ff2_pre), w_ff2_in, w_ff2_out, vec(g_ff2_post), i)
        xt = _ple(xt, pt, vec(g_ple_pre), w_gate_b, w_proj_b, vec(g_ple_post), i)
    return xt.reshape(batch, seq, D_MODEL)
```
